```python
import math
import jax, jax.numpy as jnp
from jax import lax
import numpy as np

D_MODEL = 1024
BATCH = 8
SEQ = 2048
DEPTH = 1
DEC_BATCH = 128
DEC_SEQ = 4
PAST_LEN = 8192
PAGE_SIZE = 128

N_META = 16
D_MIX = D_MODEL
HEAD_DIM = 64
ATTN_WIDTH = D_MIX // 2
N_HEADS = ATTN_WIDTH // HEAD_DIM
N_KV_HEADS = 2
GQA_GROUP = N_HEADS // N_KV_HEADS
KV_WIDTH = N_KV_HEADS * HEAD_DIM
WINDOW = 128
BLOCK = 128
SSM_WIDTH = D_MIX - ATTN_WIDTH
SSM_GROUP = 16
N_SSM_GROUPS = SSM_WIDTH // SSM_GROUP
SSM_STATE = 64
IN_COLS = ATTN_WIDTH + 2 * KV_WIDTH + SSM_WIDTH
D_FF = 2816
CONV_W = 3
EPS = 1e-5
DT_MIN = 1e-3
DT_MAX = 1e-1

kernel_name = "hymba_swa_sink_s5_convffn_step"


def rms_norm(x, g):
    xf = x.astype(jnp.float32)
    y = xf * lax.rsqrt(jnp.mean(xf * xf, axis=-1, keepdims=True) + EPS)
    return (y * g.astype(jnp.float32)).astype(x.dtype)


def sink_softmax(s, sink):
    m = jnp.maximum(jnp.max(s, axis=-1), sink)
    p = jnp.exp(s - m[..., None])
    denom = jnp.sum(p, axis=-1) + jnp.exp(sink - m)
    return p / denom[..., None]


def swa_banded(q, k, v, sinks):
    n, length = q.shape[:2]
    front = (-N_META) % BLOCK
    back = (-(front + length)) % BLOCK
    lp = front + length + back
    nb = lp // BLOCK
    pad = ((0, 0), (front, back), (0, 0), (0, 0))
    qb = jnp.pad(q, pad).reshape(n, nb, BLOCK, N_KV_HEADS, GQA_GROUP, HEAD_DIM)
    kb = jnp.pad(k, pad).reshape(n, nb, BLOCK, N_KV_HEADS, HEAD_DIM)
    vb = jnp.pad(v, pad).reshape(n, nb, BLOCK, N_KV_HEADS, HEAD_DIM)
    zero_blk = jnp.zeros_like(kb[:, :1])
    kk = jnp.concatenate([jnp.concatenate([zero_blk, kb[:, :-1]], axis=1), kb], axis=2)
    vv = jnp.concatenate([jnp.concatenate([zero_blk, vb[:, :-1]], axis=1), vb], axis=2)
    qpos = (jnp.arange(lp) - front).reshape(nb, BLOCK)
    kpos = jnp.concatenate([qpos - BLOCK, qpos], axis=1)
    mask = ((kpos[:, None, :] <= qpos[:, :, None]) & (kpos[:, None, :] > qpos[:, :, None] - WINDOW)
            & (kpos[:, None, :] >= 0) & (kpos[:, None, :] < length))
    scale = HEAD_DIM ** -0.5
    s = jnp.einsum('bnqkgd,bnskd->bnkgqs', qb.astype(jnp.float32), kk.astype(jnp.float32)) * scale
    s = jnp.where(mask[None, :, None, None], s, -jnp.inf)
    sink = sinks.astype(jnp.float32).reshape(N_KV_HEADS, GQA_GROUP)[None, None, :, :, None]
    p = sink_softmax(s, sink)
    o = jnp.einsum('bnkgqs,bnskd->bnqkgd', p, vv.astype(jnp.float32))
    o = o.reshape(n, lp, ATTN_WIDTH)[:, front:front + length]
    return o.astype(q.dtype)


def swa_cached(q, k, v, k_buf, v_buf, sinks):
    n, t = q.shape[:2]
    wb = k_buf.shape[1]
    k_all = jnp.concatenate([k_buf.astype(k.dtype), k], axis=1)
    v_all = jnp.concatenate([v_buf.astype(v.dtype), v], axis=1)
    qpos = PAST_LEN + jnp.arange(t)
    kpos = PAST_LEN - wb + jnp.arange(wb + t)
    mask = (kpos[None, :] <= qpos[:, None]) & (kpos[None, :] > qpos[:, None] - WINDOW)
    qg = q.reshape(n, t, N_KV_HEADS, GQA_GROUP, HEAD_DIM).astype(jnp.float32)
    s = jnp.einsum('btkgd,bskd->bkgts', qg, k_all.astype(jnp.float32)) * (HEAD_DIM ** -0.5)
    s = jnp.where(mask[None, None, None], s, -jnp.inf)
    sink = sinks.astype(jnp.float32).reshape(N_KV_HEADS, GQA_GROUP)[None, :, :, None]
    p = sink_softmax(s, sink)
    o = jnp.einsum('bkgts,bskd->btkgd', p, v_all.astype(jnp.float32)).reshape(n, t, ATTN_WIDTH)
    return o.astype(q.dtype), k_all, v_all


def s5_scan(u, h0, lam_re, lam_im, log_dt, b_re, b_im, c_re, c_im, d_skip):
    f32 = jnp.float32
    n, t, _ = u.shape
    lam = lax.complex(lam_re.astype(f32), lam_im.astype(f32))
    dt = jnp.exp(log_dt.astype(f32))[:, None]
    lam_bar = jnp.exp(lam * dt)
    bmat = lax.complex(b_re.astype(f32), b_im.astype(f32))
    b_bar = ((lam_bar - 1.0) / lam)[..., None] * bmat
    uf = u.astype(f32)
    ug = uf.reshape(n, t, N_SSM_GROUPS, SSM_GROUP).astype(jnp.complex64)
    bu = jnp.einsum('ntgc,gpc->ntgp', ug, b_bar)
    bu = bu.at[:, 0].add(lam_bar[None] * h0)
    a = jnp.broadcast_to(lam_bar, bu.shape)

    def combine(e1, e2):
        a1, b1 = e1
        a2, b2 = e2
        return a2 * a1, a2 * b1 + b2

    _, h = lax.associative_scan(combine, (a, bu), axis=1)
    cmat = lax.complex(c_re.astype(f32), c_im.astype(f32))
    y = jnp.real(jnp.einsum('ntgp,gcp->ntgc', h, cmat)).reshape(n, t, SSM_WIDTH)
    y = y + d_skip.astype(f32) * uf
    return y, h[:, -1]


def conv_ffn(xn, buf, w_gate, w_up, conv_w, conv_b, w_down):
    t = xn.shape[1]
    g = xn @ w_gate
    up = xn @ w_up
    gp = jnp.concatenate([buf.astype(g.dtype), g], axis=1)
    c = conv_b
    for j in range(CONV_W):
        c = c + conv_w[j] * gp[:, j:j + t]
    h = jax.nn.silu(c) * up
    return h @ w_down, gp[:, -(CONV_W - 1):]


def decoder_layer(x, k_buf, v_buf, h0, conv_buf,
                  g_mix, w_in, sinks, lam_re, lam_im, log_dt, b_re, b_im, c_re, c_im, d_skip,
                  w_glu, b_glu, g_attn_out, g_ssm_out, w_o, g_ffn, w_gate, w_up, conv_w, conv_b, w_down):
    n, t, _ = x.shape
    wb = min(WINDOW, PAST_LEN)
    xn = rms_norm(x, g_mix)
    proj = xn @ w_in
    q = proj[..., :ATTN_WIDTH].reshape(n, t, N_HEADS, HEAD_DIM)
    k = proj[..., ATTN_WIDTH:ATTN_WIDTH + KV_WIDTH].reshape(n, t, N_KV_HEADS, HEAD_DIM)
    v = proj[..., ATTN_WIDTH + KV_WIDTH:ATTN_WIDTH + 2 * KV_WIDTH].reshape(n, t, N_KV_HEADS, HEAD_DIM)
    u = proj[..., ATTN_WIDTH + 2 * KV_WIDTH:]
    if k_buf is None:
        attn = swa_banded(q, k, v, sinks)
        k_all, v_all = k, v
    else:
        attn, k_all, v_all = swa_cached(q, k, v, k_buf, v_buf, sinks)
    new_k, new_v = k_all[:, -wb:], v_all[:, -wb:]
    y_ssm, h_last = s5_scan(u, h0, lam_re, lam_im, log_dt, b_re, b_im, c_re, c_im, d_skip)
    g = jax.nn.gelu(y_ssm)
    ssm = (g * jax.nn.sigmoid(g @ w_glu.astype(jnp.float32) + b_glu.astype(jnp.float32))).astype(x.dtype)
    mixed = jnp.concatenate([rms_norm(attn, g_attn_out), rms_norm(ssm, g_ssm_out)], axis=-1) @ w_o
    x = x + mixed
    ffn, new_conv = conv_ffn(rms_norm(x, g_ffn), conv_buf, w_gate, w_up, conv_w, conv_b, w_down)
    x = x + ffn
    return x, new_k, new_v, h_last, new_conv


def setup_inputs(seed: int = 0) -> dict:
    key = jax.random.key(seed)
    ks = iter(jax.random.split(key, 40))
    f32 = jnp.float32
    wb = min(WINDOW, PAST_LEN)
    nrm = lambda shape, s: jax.random.normal(next(ks), shape, f32) * s
    gain = lambda shape: 1.0 + 0.01 * jax.random.normal(next(ks), shape, f32)
    n_idx = jnp.arange(SSM_STATE, dtype=f32)
    inp = {
        "x_prompt": nrm((BATCH, SEQ, D_MODEL), 1.0),
        "x_sample": nrm((DEC_BATCH, DEC_SEQ, D_MODEL), 1.0),
        "cache_k_win": nrm((DEPTH, DEC_BATCH, wb, N_KV_HEADS, HEAD_DIM), 1.0),
        "cache_v_win": nrm((DEPTH, DEC_BATCH, wb, N_KV_HEADS, HEAD_DIM), 1.0),
        "state_ssm_re": nrm((DEPTH, DEC_BATCH, N_SSM_GROUPS, SSM_STATE), 0.1),
        "state_ssm_im": nrm((DEPTH, DEC_BATCH, N_SSM_GROUPS, SSM_STATE), 0.1),
        "state_conv": nrm((DEPTH, DEC_BATCH, CONV_W - 1, D_FF), 1.0),
        "meta_tokens": nrm((N_META, D_MODEL), 1.0),
        "g_mix": gain((DEPTH, D_MODEL)),
        "w_in": nrm((DEPTH, D_MODEL, IN_COLS), D_MODEL ** -0.5),
        "sinks": nrm((DEPTH, N_HEADS), 0.5),
        "lam_re": -0.5 + nrm((DEPTH, N_SSM_GROUPS, SSM_STATE), 0.01),
        "lam_im": math.pi * n_idx + nrm((DEPTH, N_SSM_GROUPS, SSM_STATE), 0.01),
        "log_dt": jax.random.uniform(next(ks), (DEPTH, N_SSM_GROUPS), f32,
                                     math.log(DT_MIN), math.log(DT_MAX)),
        "b_re": nrm((DEPTH, N_SSM_GROUPS, SSM_STATE, SSM_GROUP), (2 * SSM_GROUP) ** -0.5),
        "b_im": nrm((DEPTH, N_SSM_GROUPS, SSM_STATE, SSM_GROUP), (2 * SSM_GROUP) ** -0.5),
        "c_re": nrm((DEPTH, N_SSM_GROUPS, SSM_GROUP, SSM_STATE), (2 * SSM_STATE) ** -0.5),
        "c_im": nrm((DEPTH, N_SSM_GROUPS, SSM_GROUP, SSM_STATE), (2 * SSM_STATE) ** -0.5),
        "d_skip": nrm((DEPTH, SSM_WIDTH), 1.0),
        "w_glu": nrm((DEPTH, SSM_WIDTH, SSM_WIDTH), SSM_WIDTH ** -0.5),
        "b_glu": nrm((DEPTH, SSM_WIDTH), 0.01),
        "g_attn_out": gain((DEPTH, ATTN_WIDTH)),
        "g_ssm_out": gain((DEPTH, SSM_WIDTH)),
        "w_o": nrm((DEPTH, D_MIX, D_MODEL), D_MIX ** -0.5),
        "g_ffn": gain((DEPTH, D_MODEL)),
        "w_gate": nrm((DEPTH, D_MODEL, D_FF), D_MODEL ** -0.5),
        "w_up": nrm((DEPTH, D_MODEL, D_FF), D_MODEL ** -0.5),
        "conv_w": nrm((DEPTH, CONV_W, D_FF), CONV_W ** -0.5),
        "conv_b": nrm((DEPTH, D_FF), 0.01),
        "w_down": nrm((DEPTH, D_FF, D_MODEL), D_FF ** -0.5),
        "g_final": gain((D_MODEL,)),
    }
    return inp


def reference(x_prompt, x_sample, cache_k_win, cache_v_win, state_ssm_re, state_ssm_im, state_conv,
              meta_tokens, g_mix, w_in, sinks, lam_re, lam_im, log_dt, b_re, b_im, c_re, c_im, d_skip,
              w_glu, b_glu, g_attn_out, g_ssm_out, w_o, g_ffn, w_gate, w_up, conv_w, conv_b, w_down,
              g_final):
    f32 = jnp.float32
    bsz = x_prompt.shape[0]
    meta = jnp.broadcast_to(meta_tokens.astype(x_prompt.dtype)[None], (bsz, N_META, D_MODEL))
    xp = jnp.concatenate([meta, x_prompt], axis=1)
    xs = x_sample
    pk, pv, pre, pim, pc = [], [], [], [], []
    sk, sv, sre, sim, sc = [], [], [], [], []
    for l in range(DEPTH):
        weights = (g_mix[l], w_in[l], sinks[l], lam_re[l], lam_im[l], log_dt[l], b_re[l], b_im[l],
                   c_re[l], c_im[l], d_skip[l], w_glu[l], b_glu[l], g_attn_out[l], g_ssm_out[l], w_o[l],
                   g_ffn[l], w_gate[l], w_up[l], conv_w[l], conv_b[l], w_down[l])
        h0_p = jnp.zeros((bsz, N_SSM_GROUPS, SSM_STATE), jnp.complex64)
        conv0_p = jnp.zeros((bsz, CONV_W - 1, D_FF), xp.dtype)
        xp, k_p, v_p, h_p, c_p = decoder_layer(xp, None, None, h0_p, conv0_p, *weights)
        h0_s = lax.complex(state_ssm_re[l].astype(f32), state_ssm_im[l].astype(f32))
        xs, k_s, v_s, h_s, c_s = decoder_layer(xs, cache_k_win[l], cache_v_win[l], h0_s, state_conv[l], *weights)
        pk.append(k_p); pv.append(v_p); pre.append(jnp.real(h_p)); pim.append(jnp.imag(h_p)); pc.append(c_p)
        sk.append(k_s); sv.append(v_s); sre.append(jnp.real(h_s)); sim.append(jnp.imag(h_s)); sc.append(c_s)
    y_prompt = rms_norm(xp, g_final)[:, N_META:]
    y_sample = rms_norm(xs, g_final)
    return (y_prompt, y_sample,
            jnp.stack(pk), jnp.stack(pv), jnp.stack(pre), jnp.stack(pim), jnp.stack(pc),
            jnp.stack(sk), jnp.stack(sv), jnp.stack(sre), jnp.stack(sim), jnp.stack(sc))
```

```python
import functools
import math

import jax
import jax.numpy as jnp
from jax import lax
from jax.experimental import pallas as pl
from jax.experimental.pallas import tpu as pltpu

F32 = jnp.float32
BF16 = jnp.bfloat16

D_MODEL = 1024
N_META = 16
HEAD_DIM = 64
ATTN_WIDTH = 512
N_HEADS = 8
N_KV_HEADS = 2
KV_WIDTH = 128
WINDOW = 128
SSM_WIDTH = 512
SSM_GROUP = 16
N_SSM_GROUPS = 32
SSM_STATE = 64
D_FF = 2816
EPS = 1e-5

LANES = 128
SUBLANES = 8
N_QSLABS = ATTN_WIDTH // LANES
N_PARTS = SSM_WIDTH // LANES
PART_STATE = (LANES // SSM_GROUP) * SSM_STATE
STATE_COLS = N_PARTS * 2 * PART_STATE
FFN_SLICE = 256
N_FFN_SLICES = D_FF // FFN_SLICE
QKV_COLS = ATTN_WIDTH + 2 * KV_WIDTH
VMEM_LIMIT_BYTES = 58 * 1024 * 1024

P_CHUNK = 128
P_SEQ = 8
P_ROWS = P_CHUNK * P_SEQ
P_FRONT = P_CHUNK - N_META
SCAN_SUB = 32


def _rms(x, g):
    return x * lax.rsqrt(jnp.mean(x * x, axis=-1, keepdims=True) + EPS) * g


def _dot(a, b):
    return jnp.dot(a, b, preferred_element_type=F32)


def _dot_t(a, b):
    return lax.dot_general(a, b, (((1,), (1,)), ((), ())), preferred_element_type=F32)


def _lane_is_left(shape):
    return lax.broadcasted_iota(jnp.int32, shape, len(shape) - 1) < HEAD_DIM


def _s5_scan(bu_ref, h_ref, lam_r_ref, lam_i_ref, n_steps, rows_per_step, unroll):
    n_tiles = rows_per_step // SUBLANES

    def tile_body(r, carry):
        r0 = pl.multiple_of(r * SUBLANES, SUBLANES)
        for k in range(N_PARTS):
            c_re = k * 2 * PART_STATE
            c_im = c_re + PART_STATE
            lr = jnp.broadcast_to(lam_r_ref[:, k * PART_STATE:(k + 1) * PART_STATE],
                                  (SUBLANES, PART_STATE))
            li = jnp.broadcast_to(lam_i_ref[:, k * PART_STATE:(k + 1) * PART_STATE],
                                  (SUBLANES, PART_STATE))
            hr = h_ref[pl.ds(r0, SUBLANES), c_re:c_re + PART_STATE]
            hi = h_ref[pl.ds(r0, SUBLANES), c_im:c_im + PART_STATE]

            def step(t, h):
                hr, hi = h
                row = pl.multiple_of(t * rows_per_step + r0, SUBLANES)
                br = bu_ref[pl.ds(row, SUBLANES), c_re:c_re + PART_STATE]
                bi = bu_ref[pl.ds(row, SUBLANES), c_im:c_im + PART_STATE]
                nr = lr * hr - li * hi + br
                ni = lr * hi + li * hr + bi
                bu_ref[pl.ds(row, SUBLANES), c_re:c_re + PART_STATE] = nr
                bu_ref[pl.ds(row, SUBLANES), c_im:c_im + PART_STATE] = ni
                return nr, ni

            hr, hi = lax.fori_loop(0, n_steps, step, (hr, hi), unroll=unroll)
            h_ref[pl.ds(r0, SUBLANES), c_re:c_re + PART_STATE] = hr
            h_ref[pl.ds(r0, SUBLANES), c_im:c_im + PART_STATE] = hi
        return carry

    if n_tiles == 1:
        tile_body(0, 0)
    else:
        lax.fori_loop(0, n_tiles, tile_body, 0)


def _s5_block(u_tm_ref, row0, n_rows, bu_ref, h_ref, lam_r_ref, lam_i_ref,
              bcat_ref, ccat_ref, dskip_ref, n_steps, rows_per_step, unroll):
    for k in range(N_PARTS):
        u_k = u_tm_ref[k, pl.ds(row0, n_rows), :].astype(BF16)
        bu_ref[:, k * 2 * PART_STATE:(k + 1) * 2 * PART_STATE] = _dot(u_k, bcat_ref[k])
    _s5_scan(bu_ref, h_ref, lam_r_ref, lam_i_ref, n_steps, rows_per_step, unroll)
    for k in range(N_PARTS):
        h_k = bu_ref[:, k * 2 * PART_STATE:(k + 1) * 2 * PART_STATE].astype(BF16)
        y_k = _dot(h_k, ccat_ref[k])
        u_k = u_tm_ref[k, pl.ds(row0, n_rows), :]
        u_tm_ref[k, pl.ds(row0, n_rows), :] = y_k + dskip_ref[:, k * LANES:(k + 1) * LANES] * u_k


def _glu_norm(y, w_glu_ref, b_glu_ref, g_ssm_ref):
    g = jax.nn.gelu(y)
    gate = jax.nn.sigmoid(_dot(g.astype(BF16), w_glu_ref[...]) + b_glu_ref[...])
    return _rms(g * gate, g_ssm_ref[...])


def _softmax_sink(parts, sink):
    m = sink
    for s in parts:
        m = jnp.maximum(m, jnp.max(s, axis=-1, keepdims=True))
    ps = [jnp.exp(s - m) for s in parts]
    denom = jnp.exp(sink - m)
    for p in ps:
        denom = denom + jnp.sum(p, axis=-1, keepdims=True)
    return ps, denom


def _prompt_mixer_kernel(x_ref, meta_ref, g_mix_ref, w_in_ref, sink_ref, lam_r_ref, lam_i_ref,
                         bcat_ref, ccat_ref, dskip_ref, w_glu_ref, b_glu_ref, g_attn_ref,
                         g_ssm_ref, wo_a_ref, wo_s_ref,
                         x1_ref, k_out_ref, v_out_ref, h_ref,
                         qkv_sc, kvprev_sc, u_tm, bu_sc, o_sc, xt_sc):
    i = pl.program_id(0)

    @pl.when(i == 0)
    def _():
        h_ref[...] = jnp.zeros_like(h_ref)
        kvprev_sc[...] = jnp.zeros_like(kvprev_sc)

    def load_x():
        first = jnp.concatenate([jnp.zeros((P_FRONT, D_MODEL), F32), meta_ref[...]], axis=0)
        first = jnp.broadcast_to(first[None], (P_SEQ, P_CHUNK, D_MODEL))
        return jnp.where(i == 0, first, x_ref[...]).reshape(P_ROWS, D_MODEL)

    xn = _rms(load_x(), g_mix_ref[...]).astype(BF16)
    qkv = _dot(xn, w_in_ref[:, :QKV_COLS])
    k_out_ref[...] = qkv[:, ATTN_WIDTH:ATTN_WIDTH + KV_WIDTH].reshape(P_SEQ, P_CHUNK, KV_WIDTH)
    v_out_ref[...] = qkv[:, ATTN_WIDTH + KV_WIDTH:].reshape(P_SEQ, P_CHUNK, KV_WIDTH)
    qkv_sc[...] = qkv.astype(BF16)
    u = _dot(xn, w_in_ref[:, QKV_COLS:])
    for s in range(P_SEQ):
        for k in range(N_PARTS):
            u_tm[k, pl.ds(s, P_CHUNK, stride=P_SEQ), :] = (
                u[s * P_CHUNK:(s + 1) * P_CHUNK, k * LANES:(k + 1) * LANES])

    tq = lax.broadcasted_iota(jnp.int32, (2 * P_CHUNK, 2 * P_CHUNK), 0) % P_CHUNK
    col = lax.broadcasted_iota(jnp.int32, (2 * P_CHUNK, 2 * P_CHUNK), 1)
    is_prev = col < P_CHUNK
    tk = jnp.where(is_prev, col, col - P_CHUNK)
    kpos = jnp.where(is_prev, (i - 1) * P_CHUNK, i * P_CHUNK) + tk - P_FRONT
    valid = ((is_prev & (tk > tq)) | (jnp.logical_not(is_prev) & (tk <= tq))) & (kpos >= 0)
    bias = jnp.where(valid, 0.0, -jnp.inf).astype(F32)
    left = _lane_is_left((P_CHUNK, LANES))

    def attn_seq(s, carry):
        r0 = pl.multiple_of(s * P_CHUNK, P_CHUNK)
        kv_cur = qkv_sc[pl.ds(r0, P_CHUNK), ATTN_WIDTH:]
        kv_cat = jnp.concatenate([kvprev_sc[pl.ds(r0, P_CHUNK), :], kv_cur], axis=0)
        k_cat = kv_cat[:, :KV_WIDTH]
        v_cat = kv_cat[:, KV_WIDTH:]
        for j in range(N_QSLABS):
            q = qkv_sc[pl.ds(r0, P_CHUNK), j * LANES:(j + 1) * LANES]
            q2 = jnp.concatenate([jnp.where(left, q, 0), jnp.where(left, 0, q)], axis=0)
            sc = _dot_t(q2, k_cat) + bias
            (p,), denom = _softmax_sink([sc], sink_ref[j])
            o2 = _dot(p.astype(BF16), v_cat) / denom
            o_sc[pl.ds(r0, P_CHUNK), j * LANES:(j + 1) * LANES] = (
                jnp.where(left, o2[:P_CHUNK], o2[P_CHUNK:]))
        return carry

    lax.fori_loop(0, P_SEQ, attn_seq, 0)
    kvprev_sc[...] = qkv_sc[:, ATTN_WIDTH:]

    sub_rows = SCAN_SUB * P_SEQ

    def s5_sub(c, carry):
        row0 = pl.multiple_of(c * sub_rows, sub_rows)
        _s5_block(u_tm, row0, sub_rows, bu_sc, h_ref, lam_r_ref, lam_i_ref, bcat_ref, ccat_ref,
                  dskip_ref, SCAN_SUB, P_SEQ, 4)
        return carry

    lax.fori_loop(0, P_CHUNK // SCAN_SUB, s5_sub, 0)
    y_ssm = jnp.concatenate([u_tm[k] for k in range(N_PARTS)], axis=1)
    ssm_n = _glu_norm(y_ssm, w_glu_ref, b_glu_ref, g_ssm_ref)
    mix_tm = _dot(ssm_n.astype(BF16), wo_s_ref[...])

    attn_n = _rms(o_sc[...], g_attn_ref[...])
    x1_bm = load_x() + _dot(attn_n.astype(BF16), wo_a_ref[...])
    for s in range(P_SEQ):
        for j in range(D_MODEL // LANES):
            xt_sc[j, pl.ds(s, P_CHUNK, stride=P_SEQ), :] = (
                x1_bm[s * P_CHUNK:(s + 1) * P_CHUNK, j * LANES:(j + 1) * LANES])
    x1_ref[...] = jnp.concatenate([xt_sc[j] for j in range(D_MODEL // LANES)], axis=1) + mix_tm


def _ffn_kernel(x1_ref, conv0_ref, g_ffn_ref, w_gate_ref, w_up_ref, conv_w_ref, conv_b_ref,
                w_down_ref, g_final_ref, y_ref, conv_ref, h_sc, *rest, rows_per_step, to_batch_major):
    i = pl.program_id(0)
    n_rows = x1_ref.shape[0]
    halo = 2 * rows_per_step

    @pl.when(i == 0)
    def _():
        conv_ref[...] = conv0_ref[...]

    x1 = x1_ref[...]
    xn = _rms(x1, g_ffn_ref[...]).astype(BF16)
    for sl in range(N_FFN_SLICES):
        c0, c1 = sl * FFN_SLICE, (sl + 1) * FFN_SLICE
        g = _dot(xn, w_gate_ref[:, c0:c1])
        up = _dot(xn, w_up_ref[:, c0:c1])
        gp = jnp.concatenate([conv_ref[:, c0:c1], g], axis=0)
        conv = (conv_b_ref[:, c0:c1]
                + conv_w_ref[0:1, c0:c1] * gp[0:n_rows]
                + conv_w_ref[1:2, c0:c1] * gp[rows_per_step:rows_per_step + n_rows]
                + conv_w_ref[2:3, c0:c1] * gp[halo:halo + n_rows])
        conv_ref[:, c0:c1] = gp[n_rows:n_rows + halo]
        h_sc[:, c0:c1] = (jax.nn.silu(conv) * up).astype(BF16)
    x2 = x1 + _dot(h_sc[...], w_down_ref[...])
    y = _rms(x2, g_final_ref[...])
    if to_batch_major:
        (yt_sc,) = rest
        n_seq = rows_per_step
        n_t = n_rows // n_seq
        for j in range(D_MODEL // LANES):
            yt_sc[j] = y[:, j * LANES:(j + 1) * LANES]
        for s in range(n_seq):
            for j in range(D_MODEL // LANES):
                y_ref[s, :, j * LANES:(j + 1) * LANES] = yt_sc[j, pl.ds(s, n_t, stride=n_seq), :]
    else:
        y_ref[...] = y


def _sample_inproj_kernel(x_ref, h0_ref, g_mix_ref, w_in_ref, lam_r_ref, lam_i_ref, bcat_ref,
                          ccat_ref, dskip_ref, w_glu_ref, b_glu_ref, g_ssm_ref,
                          q_ref, k_ref, v_ref, ssm_ref, h_ref, u_tm, bu_sc, *, n_steps, n_seq):
    n_rows = n_steps * n_seq
    xn = _rms(x_ref[...], g_mix_ref[...]).astype(BF16)
    qkv = _dot(xn, w_in_ref[:, :QKV_COLS])
    q_ref[...] = qkv[:, :ATTN_WIDTH].astype(BF16)
    k_ref[...] = qkv[:, ATTN_WIDTH:ATTN_WIDTH + KV_WIDTH]
    v_ref[...] = qkv[:, ATTN_WIDTH + KV_WIDTH:]
    u = _dot(xn, w_in_ref[:, QKV_COLS:])
    for k in range(N_PARTS):
        u_tm[k] = u[:, k * LANES:(k + 1) * LANES]
    h_ref[...] = h0_ref[...]
    _s5_block(u_tm, 0, n_rows, bu_sc, h_ref, lam_r_ref, lam_i_ref, bcat_ref, ccat_ref,
              dskip_ref, n_steps, n_seq, True)
    y_ssm = jnp.concatenate([u_tm[k] for k in range(N_PARTS)], axis=1)
    ssm_ref[...] = _glu_norm(y_ssm, w_glu_ref, b_glu_ref, g_ssm_ref).astype(BF16)


def _sample_attn_kernel(q_ref, kn_ref, vn_ref, kc_ref, vc_ref, sink_ref, x_ref, ssm_ref,
                        g_attn_ref, wo_a_ref, wo_s_ref, x1_ref, o_sc, bias_c_sc, bias_n_sc,
                        *, n_steps, n_seq, seq_batch, n_cache):
    b = pl.program_id(0)
    rows = n_steps * seq_batch
    q_rows = N_QSLABS * rows

    @pl.when(b == 0)
    def _():
        r = lax.broadcasted_iota(jnp.int32, (q_rows, seq_batch * n_cache), 0)
        c = lax.broadcasted_iota(jnp.int32, (q_rows, seq_batch * n_cache), 1)
        t = (r % rows) // seq_batch
        ok = ((r % seq_batch) == (c // n_cache)) & ((c % n_cache) > t)
        bias_c_sc[...] = jnp.where(ok, 0.0, -jnp.inf).astype(F32)
        r = lax.broadcasted_iota(jnp.int32, (q_rows, rows), 0)
        c = lax.broadcasted_iota(jnp.int32, (q_rows, rows), 1)
        t = (r % rows) // seq_batch
        ok = ((r % seq_batch) == (c % seq_batch)) & ((c // seq_batch) <= t)
        bias_n_sc[...] = jnp.where(ok, 0.0, -jnp.inf).astype(F32)

    q = q_ref[...].reshape(rows, ATTN_WIDTH)
    kc = kc_ref[...].astype(BF16)
    vc = vc_ref[...].astype(BF16)
    kn = kn_ref[...].reshape(rows, KV_WIDTH).astype(BF16)
    vn = vn_ref[...].reshape(rows, KV_WIDTH).astype(BF16)
    left = _lane_is_left((rows, LANES))
    halves = []
    for grp in range(N_KV_HEADS):
        qs = []
        for j in range(N_QSLABS):
            qj = q[:, j * LANES:(j + 1) * LANES]
            qs.append(jnp.where(left, qj, 0) if grp == 0 else jnp.where(left, 0, qj))
        qg = jnp.concatenate(qs, axis=0)
        sc_c = _dot_t(qg, kc) + bias_c_sc[...]
        sc_n = _dot_t(qg, kn) + bias_n_sc[...]
        (p_c, p_n), denom = _softmax_sink([sc_c, sc_n], sink_ref[grp])
        halves.append((_dot(p_c.astype(BF16), vc) + _dot(p_n.astype(BF16), vn)) / denom)
    for j in range(N_QSLABS):
        o_j = jnp.where(left, halves[0][j * rows:(j + 1) * rows], halves[1][j * rows:(j + 1) * rows])
        s0 = pl.multiple_of(b * seq_batch, seq_batch)
        o_sc[:, pl.ds(s0, seq_batch), j * LANES:(j + 1) * LANES] = (
            o_j.reshape(n_steps, seq_batch, LANES))

    @pl.when(b == pl.num_programs(0) - 1)
    def _():
        attn_n = _rms(o_sc[...].reshape(n_steps * n_seq, ATTN_WIDTH), g_attn_ref[...])
        x1_ref[...] = (x_ref[...] + _dot(attn_n.astype(BF16), wo_a_ref[...])
                       + _dot(ssm_ref[...], wo_s_ref[...]))


def _full(shape):
    return pl.BlockSpec(shape, lambda *_: (0,) * len(shape))


def _const(shape):
    return pl.BlockSpec(shape, lambda *_: (0,) * len(shape), pipeline_mode=pl.Buffered(1))


def _params(semantics):
    return pltpu.CompilerParams(dimension_semantics=semantics, vmem_limit_bytes=VMEM_LIMIT_BYTES)


def _ffn_call(x1_tm, conv0, w, *, rows_per_step, block_rows, to_batch_major):
    n_rows = x1_tm.shape[0]
    n_blocks = n_rows // block_rows
    halo = 2 * rows_per_step
    in_specs = [
        pl.BlockSpec((block_rows, D_MODEL), lambda i: (i, 0)),
        _const((halo, D_FF)), _const((1, D_MODEL)), _const((D_MODEL, D_FF)), _const((D_MODEL, D_FF)),
        _const((3, D_FF)), _const((1, D_FF)), _const((D_FF, D_MODEL)), _const((1, D_MODEL)),
    ]
    scratch = [pltpu.VMEM((block_rows, D_FF), BF16)]
    if to_batch_major:
        n_t = block_rows // rows_per_step
        y_shape = jax.ShapeDtypeStruct((rows_per_step, (n_blocks - 1) * n_t, D_MODEL), F32)
        y_spec = pl.BlockSpec((rows_per_step, n_t, D_MODEL), lambda i: (0, jnp.maximum(i - 1, 0), 0))
        scratch.append(pltpu.VMEM((D_MODEL // LANES, block_rows, LANES), F32))
    else:
        y_shape = jax.ShapeDtypeStruct((n_rows, D_MODEL), F32)
        y_spec = pl.BlockSpec((block_rows, D_MODEL), lambda i: (i, 0))
    return pl.pallas_call(
        functools.partial(_ffn_kernel, rows_per_step=rows_per_step, to_batch_major=to_batch_major),
        grid=(n_blocks,),
        in_specs=in_specs,
        out_specs=[y_spec, _full((halo, D_FF))],
        out_shape=[y_shape, jax.ShapeDtypeStruct((halo, D_FF), F32)],
        scratch_shapes=scratch,
        compiler_params=_params(("arbitrary",)),
        name="conv_ffn",
    )(x1_tm, conv0, w["g_ffn"], w["w_gate"], w["w_up"], w["conv_w"], w["conv_b"], w["w_down"],
      w["g_final"])


def _prep_weights(g_mix, w_in, sinks, lam_re, lam_im, log_dt, b_re, b_im, c_re, c_im, d_skip,
                  w_glu, b_glu, g_attn_out, g_ssm_out, w_o, g_ffn, w_gate, w_up, conv_w, conv_b,
                  w_down, g_final):
    group = N_HEADS // N_KV_HEADS
    perm = jnp.concatenate([
        jnp.concatenate([jnp.arange(HEAD_DIM) + j * HEAD_DIM,
                         jnp.arange(HEAD_DIM) + (group + j) * HEAD_DIM]) for j in range(group)])
    w_q = w_in[:, :ATTN_WIDTH][:, perm] * (HEAD_DIM ** -0.5)
    w = {
        "g_mix": g_mix.reshape(1, D_MODEL),
        "w_in": jnp.concatenate([w_q, w_in[:, ATTN_WIDTH:]], axis=1).astype(BF16),
        "g_attn": g_attn_out[perm].reshape(1, ATTN_WIDTH),
        "g_ssm": g_ssm_out.reshape(1, SSM_WIDTH),
        "wo_a": w_o[:ATTN_WIDTH][perm].astype(BF16),
        "wo_s": w_o[ATTN_WIDTH:].astype(BF16),
        "w_glu": w_glu.astype(BF16),
        "b_glu": b_glu.reshape(1, SSM_WIDTH),
        "dskip": d_skip.reshape(1, SSM_WIDTH),
        "g_ffn": g_ffn.reshape(1, D_MODEL),
        "w_gate": w_gate.astype(BF16),
        "w_up": w_up.astype(BF16),
        "conv_w": conv_w,
        "conv_b": conv_b.reshape(1, D_FF),
        "w_down": w_down.astype(BF16),
        "g_final": g_final.reshape(1, D_MODEL),
    }
    lam = lax.complex(lam_re, lam_im)
    lam_bar = jnp.exp(lam * jnp.exp(log_dt)[:, None])
    b_bar = ((lam_bar - 1.0) / lam)[..., None] * lax.complex(b_re, b_im)
    w["lam_r"] = jnp.real(lam_bar).reshape(1, N_SSM_GROUPS * SSM_STATE)
    w["lam_i"] = jnp.imag(lam_bar).reshape(1, N_SSM_GROUPS * SSM_STATE)
    gpp = LANES // SSM_GROUP
    eye = jnp.eye(gpp, dtype=F32)

    def b_part(b):
        b = b.reshape(N_PARTS, gpp, SSM_STATE, SSM_GROUP)
        return jnp.einsum("kgpc,gh->kgchp", b, eye).reshape(N_PARTS, LANES, PART_STATE)

    def c_part(c):
        c = c.reshape(N_PARTS, gpp, SSM_GROUP, SSM_STATE)
        return jnp.einsum("kgcp,gh->kgphc", c, eye).reshape(N_PARTS, PART_STATE, LANES)

    w["bcat"] = jnp.concatenate([b_part(jnp.real(b_bar)), b_part(jnp.imag(b_bar))], axis=2).astype(BF16)
    w["ccat"] = jnp.concatenate([c_part(c_re), -c_part(c_im)], axis=1).astype(BF16)
    w["sink_prompt"] = jnp.stack([jnp.concatenate([jnp.full((P_CHUNK, 1), 1.0) * sinks[j],
                                                   jnp.full((P_CHUNK, 1), 1.0) * sinks[group + j]])
                                  for j in range(group)]).astype(F32)
    w["sinks"] = sinks
    return w


def _state_to_cols(re, im):
    n = re.shape[0]
    st = jnp.stack([re.reshape(n, N_PARTS, PART_STATE), im.reshape(n, N_PARTS, PART_STATE)], axis=2)
    return st.reshape(n, STATE_COLS)


def _cols_to_state(h):
    n = h.shape[0]
    st = h.reshape(n, N_PARTS, 2, PART_STATE)
    re = st[:, :, 0].reshape(n, N_SSM_GROUPS, SSM_STATE)
    im = st[:, :, 1].reshape(n, N_SSM_GROUPS, SSM_STATE)
    return re, im


def _prompt_path(x_prompt, meta_tokens, w):
    n_seq, seq_len, _ = x_prompt.shape
    assert n_seq == P_SEQ and seq_len % P_CHUNK == 0
    n_chunks = seq_len // P_CHUNK + 1
    ins = [x_prompt, meta_tokens, w["g_mix"], w["w_in"], w["sink_prompt"], w["lam_r"], w["lam_i"],
           w["bcat"], w["ccat"], w["dskip"], w["w_glu"], w["b_glu"], w["g_attn"], w["g_ssm"],
           w["wo_a"], w["wo_s"]]
    in_specs = [pl.BlockSpec((P_SEQ, P_CHUNK, D_MODEL), lambda i: (0, jnp.maximum(i - 1, 0), 0))]
    in_specs += [_const(a.shape) for a in ins[1:]]
    x1_tm, k_last, v_last, h_last = pl.pallas_call(
        _prompt_mixer_kernel,
        grid=(n_chunks,),
        in_specs=in_specs,
        out_specs=[pl.BlockSpec((P_ROWS, D_MODEL), lambda i: (i, 0)),
                   _full((P_SEQ, P_CHUNK, KV_WIDTH)), _full((P_SEQ, P_CHUNK, KV_WIDTH)),
                   _full((P_SEQ, STATE_COLS))],
        out_shape=[jax.ShapeDtypeStruct((n_chunks * P_ROWS, D_MODEL), F32),
                   jax.ShapeDtypeStruct((P_SEQ, P_CHUNK, KV_WIDTH), F32),
                   jax.ShapeDtypeStruct((P_SEQ, P_CHUNK, KV_WIDTH), F32),
                   jax.ShapeDtypeStruct((P_SEQ, STATE_COLS), F32)],
        scratch_shapes=[
            pltpu.VMEM((P_ROWS, QKV_COLS), BF16),
            pltpu.VMEM((P_ROWS, 2 * KV_WIDTH), BF16),
            pltpu.VMEM((N_PARTS, P_ROWS, LANES), F32),
            pltpu.VMEM((SCAN_SUB * P_SEQ, STATE_COLS), F32),
            pltpu.VMEM((P_ROWS, ATTN_WIDTH), F32),
            pltpu.VMEM((D_MODEL // LANES, P_ROWS, LANES), F32),
        ],
        compiler_params=_params(("arbitrary",)),
        name="prompt_mixer",
    )(*ins)
    conv0 = jnp.zeros((2 * P_SEQ, D_FF), F32)
    y_prompt, conv_tm = _ffn_call(x1_tm, conv0, w, rows_per_step=P_SEQ, block_rows=P_ROWS,
                                  to_batch_major=True)
    p_re, p_im = _cols_to_state(h_last)
    p_conv = conv_tm.reshape(2, P_SEQ, D_FF).transpose(1, 0, 2)
    kv_shape = (1, P_SEQ, WINDOW, N_KV_HEADS, HEAD_DIM)
    return (y_prompt, k_last.reshape(kv_shape), v_last.reshape(kv_shape), p_re[None], p_im[None],
            p_conv[None])


def _sample_path(x_sample, cache_k, cache_v, st_re, st_im, st_conv, w):
    n_seq, n_steps, _ = x_sample.shape
    n_cache = cache_k.shape[1]
    n_rows = n_steps * n_seq
    seq_batch = 16
    x_tm = x_sample.transpose(1, 0, 2).reshape(n_rows, D_MODEL)
    h0 = _state_to_cols(st_re, st_im)
    ins = [x_tm, h0, w["g_mix"], w["w_in"], w["lam_r"], w["lam_i"], w["bcat"], w["ccat"],
           w["dskip"], w["w_glu"], w["b_glu"], w["g_ssm"]]
    q, k_new, v_new, ssm_n, h_last = pl.pallas_call(
        functools.partial(_sample_inproj_kernel, n_steps=n_steps, n_seq=n_seq),
        grid=(1,),
        in_specs=[_full(a.shape) for a in ins],
        out_specs=[_full((n_rows, ATTN_WIDTH)), _full((n_rows, KV_WIDTH)), _full((n_rows, KV_WIDTH)),
                   _full((n_rows, SSM_WIDTH)), _full((n_seq, STATE_COLS))],
        out_shape=[jax.ShapeDtypeStruct((n_rows, ATTN_WIDTH), BF16),
                   jax.ShapeDtypeStruct((n_rows, KV_WIDTH), F32),
                   jax.ShapeDtypeStruct((n_rows, KV_WIDTH), F32),
                   jax.ShapeDtypeStruct((n_rows, SSM_WIDTH), BF16),
                   jax.ShapeDtypeStruct((n_seq, STATE_COLS), F32)],
        scratch_shapes=[pltpu.VMEM((N_PARTS, n_rows, LANES), F32),
                        pltpu.VMEM((n_rows, STATE_COLS), F32)],
        compiler_params=_params(("arbitrary",)),
        name="sample_inproj",
    )(*ins)

    group = N_HEADS // N_KV_HEADS
    rows = n_steps * seq_batch
    sink_sample = jnp.stack([
        jnp.concatenate([jnp.full((rows, 1), 1.0) * w["sinks"][g * group + j] for j in range(group)])
        for g in range(N_KV_HEADS)]).astype(F32)
    kc = cache_k.reshape(n_seq * n_cache, KV_WIDTH)
    vc = cache_v.reshape(n_seq * n_cache, KV_WIDTH)
    step_block = lambda width: pl.BlockSpec((n_steps, seq_batch, width), lambda b: (0, b, 0))
    cache_block = pl.BlockSpec((seq_batch * n_cache, KV_WIDTH), lambda b: (b, 0))
    ins = [q.reshape(n_steps, n_seq, ATTN_WIDTH), k_new.reshape(n_steps, n_seq, KV_WIDTH),
           v_new.reshape(n_steps, n_seq, KV_WIDTH), kc, vc, sink_sample, x_tm, ssm_n,
           w["g_attn"], w["wo_a"], w["wo_s"]]
    in_specs = [step_block(ATTN_WIDTH), step_block(KV_WIDTH), step_block(KV_WIDTH), cache_block,
                cache_block] + [_const(a.shape) for a in ins[5:]]
    x1_tm = pl.pallas_call(
        functools.partial(_sample_attn_kernel, n_steps=n_steps, n_seq=n_seq, seq_batch=seq_batch,
                          n_cache=n_cache),
        grid=(n_seq // seq_batch,),
        in_specs=in_specs,
        out_specs=_full((n_rows, D_MODEL)),
        out_shape=jax.ShapeDtypeStruct((n_rows, D_MODEL), F32),
        scratch_shapes=[pltpu.VMEM((n_steps, n_seq, ATTN_WIDTH), F32),
                        pltpu.VMEM((N_QSLABS * rows, seq_batch * n_cache), F32),
                        pltpu.VMEM((N_QSLABS * rows, rows), F32)],
        compiler_params=_params(("arbitrary",)),
        name="sample_attn",
    )(*ins)

    conv0 = st_conv.transpose(1, 0, 2).reshape(2 * n_seq, D_FF)
    y_tm, conv_tm = _ffn_call(x1_tm, conv0, w, rows_per_step=n_seq, block_rows=n_rows,
                              to_batch_major=False)
    y_sample = y_tm.reshape(n_steps, n_seq, D_MODEL).transpose(1, 0, 2)
    s_conv = conv_tm.reshape(2, n_seq, D_FF).transpose(1, 0, 2)
    s_re, s_im = _cols_to_state(h_last)
    kv_shape = (n_seq, n_steps, N_KV_HEADS, HEAD_DIM)
    k_new = k_new.reshape(n_steps, n_seq, KV_WIDTH).transpose(1, 0, 2).reshape(kv_shape)
    v_new = v_new.reshape(n_steps, n_seq, KV_WIDTH).transpose(1, 0, 2).reshape(kv_shape)
    s_k = jnp.concatenate([cache_k[:, n_steps:], k_new], axis=1)
    s_v = jnp.concatenate([cache_v[:, n_steps:], v_new], axis=1)
    return y_sample, s_k[None], s_v[None], s_re[None], s_im[None], s_conv[None]


def kernel(x_prompt, x_sample, cache_k_win, cache_v_win, state_ssm_re, state_ssm_im, state_conv, meta_tokens, g_mix, w_in, sinks, lam_re, lam_im, log_dt, b_re, b_im, c_re, c_im, d_skip, w_glu, b_glu, g_attn_out, g_ssm_out, w_o, g_ffn, w_gate, w_up, conv_w, conv_b, w_down, g_final):
    assert g_mix.shape[0] == 1, "single-layer kernel"
    w = _prep_weights(g_mix[0], w_in[0], sinks[0], lam_re[0], lam_im[0], log_dt[0], b_re[0], b_im[0],
                      c_re[0], c_im[0], d_skip[0], w_glu[0], b_glu[0], g_attn_out[0], g_ssm_out[0],
                      w_o[0], g_ffn[0], w_gate[0], w_up[0], conv_w[0], conv_b[0], w_down[0], g_final)
    y_p, p_k, p_v, p_re, p_im, p_conv = _prompt_path(x_prompt, meta_tokens, w)
    y_s, s_k, s_v, s_re, s_im, s_conv = _sample_path(
        x_sample, cache_k_win[0], cache_v_win[0], state_ssm_re[0], state_ssm_im[0], state_conv[0], w)
    return (y_p, y_s, p_k, p_v, p_re, p_im, p_conv, s_k, s_v, s_re, s_im, s_conv)
```

```python
import functools
import math

import jax
import jax.numpy as jnp
from jax import lax
from jax.experimental import pallas as pl
from jax.experimental.pallas import tpu as pltpu

F32 = jnp.float32
BF16 = jnp.bfloat16

D_MODEL = 1024
N_META = 16
HEAD_DIM = 64
ATTN_WIDTH = 512
N_HEADS = 8
N_KV_HEADS = 2
KV_WIDTH = 128
WINDOW = 128
SSM_WIDTH = 512
SSM_GROUP = 16
N_SSM_GROUPS = 32
SSM_STATE = 64
D_FF = 2816
EPS = 1e-5

LANES = 128
SUBLANES = 8
N_QSLABS = ATTN_WIDTH // LANES
N_PARTS = SSM_WIDTH // LANES
PART_STATE = (LANES // SSM_GROUP) * SSM_STATE
STATE_COLS = N_PARTS * 2 * PART_STATE
FFN_SLICE = 256
N_FFN_SLICES = D_FF // FFN_SLICE
QKV_COLS = ATTN_WIDTH + 2 * KV_WIDTH
VMEM_LIMIT_BYTES = 58 * 1024 * 1024

P_CHUNK = 128
P_SEQ = 8
P_ROWS = P_CHUNK * P_SEQ
P_FRONT = P_CHUNK - N_META
SCAN_SUB = 32


def _rms(x, g):
    return x * lax.rsqrt(jnp.mean(x * x, axis=-1, keepdims=True) + EPS) * g


def _dot(a, b):
    return jnp.dot(a, b, preferred_element_type=F32)


def _dot_t(a, b):
    return lax.dot_general(a, b, (((1,), (1,)), ((), ())), preferred_element_type=F32)


def _lane_is_left(shape):
    return lax.broadcasted_iota(jnp.int32, shape, len(shape) - 1) < HEAD_DIM


def _s5_scan(bu_ref, h_ref, lam_r_ref, lam_i_ref, n_steps, rows_per_step, unroll):
    n_tiles = rows_per_step // SUBLANES

    def tile_body(r, carry):
        r0 = pl.multiple_of(r * SUBLANES, SUBLANES)
        for k in range(N_PARTS):
            c_re = k * 2 * PART_STATE
            c_im = c_re + PART_STATE
            lr = jnp.broadcast_to(lam_r_ref[:, k * PART_STATE:(k + 1) * PART_STATE],
                                  (SUBLANES, PART_STATE))
            li = jnp.broadcast_to(lam_i_ref[:, k * PART_STATE:(k + 1) * PART_STATE],
                                  (SUBLANES, PART_STATE))
            hr = h_ref[pl.ds(r0, SUBLANES), c_re:c_re + PART_STATE]
            hi = h_ref[pl.ds(r0, SUBLANES), c_im:c_im + PART_STATE]

            def step(t, h):
                hr, hi = h
                row = pl.multiple_of(t * rows_per_step + r0, SUBLANES)
                br = bu_ref[pl.ds(row, SUBLANES), c_re:c_re + PART_STATE]
                bi = bu_ref[pl.ds(row, SUBLANES), c_im:c_im + PART_STATE]
                nr = lr * hr - li * hi + br
                ni = lr * hi + li * hr + bi
                bu_ref[pl.ds(row, SUBLANES), c_re:c_re + PART_STATE] = nr
                bu_ref[pl.ds(row, SUBLANES), c_im:c_im + PART_STATE] = ni
                return nr, ni

            hr, hi = lax.fori_loop(0, n_steps, step, (hr, hi), unroll=unroll)
            h_ref[pl.ds(r0, SUBLANES), c_re:c_re + PART_STATE] = hr
            h_ref[pl.ds(r0, SUBLANES), c_im:c_im + PART_STATE] = hi
        return carry

    if n_tiles == 1:
        tile_body(0, 0)
    else:
        lax.fori_loop(0, n_tiles, tile_body, 0)


def _s5_block(u_tm_ref, row0, n_rows, bu_ref, h_ref, lam_r_ref, lam_i_ref,
              bcat_ref, ccat_ref, dskip_ref, n_steps, rows_per_step, unroll):
    for k in range(N_PARTS):
        u_k = u_tm_ref[k, pl.ds(row0, n_rows), :].astype(BF16)
        bu_ref[:, k * 2 * PART_STATE:(k + 1) * 2 * PART_STATE] = _dot(u_k, bcat_ref[k])
    _s5_scan(bu_ref, h_ref, lam_r_ref, lam_i_ref, n_steps, rows_per_step, unroll)
    for k in range(N_PARTS):
        h_k = bu_ref[:, k * 2 * PART_STATE:(k + 1) * 2 * PART_STATE].astype(BF16)
        y_k = _dot(h_k, ccat_ref[k])
        u_k = u_tm_ref[k, pl.ds(row0, n_rows), :]
        u_tm_ref[k, pl.ds(row0, n_rows), :] = y_k + dskip_ref[:, k * LANES:(k + 1) * LANES] * u_k


def _glu_norm(y, w_glu_ref, b_glu_ref, g_ssm_ref):
    g = jax.nn.gelu(y)
    gate = jax.nn.sigmoid(_dot(g.astype(BF16), w_glu_ref[...]) + b_glu_ref[...])
    return _rms(g * gate, g_ssm_ref[...])


def _softmax_sink(parts, sink):
    m = sink
    for s in parts:
        m = jnp.maximum(m, jnp.max(s, axis=-1, keepdims=True))
    ps = [jnp.exp(s - m) for s in parts]
    denom = jnp.exp(sink - m)
    for p in ps:
        denom = denom + jnp.sum(p, axis=-1, keepdims=True)
    return ps, denom


def _prompt_mixer_kernel(x_ref, meta_ref, g_mix_ref, w_in_ref, sink_ref, lam_r_ref, lam_i_ref,
                         bcat_ref, ccat_ref, dskip_ref, w_glu_ref, b_glu_ref, g_attn_ref,
                         g_ssm_ref, wo_a_ref, wo_s_ref,
                         x1_ref, k_out_ref, v_out_ref, h_ref,
                         qkv_sc, kvprev_sc, u_tm, bu_sc, o_sc, xt_sc):
    i = pl.program_id(0)

    @pl.when(i == 0)
    def _():
        h_ref[...] = jnp.zeros_like(h_ref)
        kvprev_sc[...] = jnp.zeros_like(kvprev_sc)

    def load_x():
        first = jnp.concatenate([jnp.zeros((P_FRONT, D_MODEL), F32), meta_ref[...]], axis=0)
        first = jnp.broadcast_to(first[None], (P_SEQ, P_CHUNK, D_MODEL))
        return jnp.where(i == 0, first, x_ref[...]).reshape(P_ROWS, D_MODEL)

    xn = _rms(load_x(), g_mix_ref[...]).astype(BF16)
    qkv = _dot(xn, w_in_ref[:, :QKV_COLS])
    k_out_ref[...] = qkv[:, ATTN_WIDTH:ATTN_WIDTH + KV_WIDTH].reshape(P_SEQ, P_CHUNK, KV_WIDTH)
    v_out_ref[...] = qkv[:, ATTN_WIDTH + KV_WIDTH:].reshape(P_SEQ, P_CHUNK, KV_WIDTH)
    qkv_sc[...] = qkv.astype(BF16)
    u = _dot(xn, w_in_ref[:, QKV_COLS:])
    for s in range(P_SEQ):
        for k in range(N_PARTS):
            u_tm[k, pl.ds(s, P_CHUNK, stride=P_SEQ), :] = (
                u[s * P_CHUNK:(s + 1) * P_CHUNK, k * LANES:(k + 1) * LANES])

    tq = lax.broadcasted_iota(jnp.int32, (2 * P_CHUNK, 2 * P_CHUNK), 0) % P_CHUNK
    col = lax.broadcasted_iota(jnp.int32, (2 * P_CHUNK, 2 * P_CHUNK), 1)
    is_prev = col < P_CHUNK
    tk = jnp.where(is_prev, col, col - P_CHUNK)
    kpos = jnp.where(is_prev, (i - 1) * P_CHUNK, i * P_CHUNK) + tk - P_FRONT
    valid = ((is_prev & (tk > tq)) | (jnp.logical_not(is_prev) & (tk <= tq))) & (kpos >= 0)
    bias = jnp.where(valid, 0.0, -jnp.inf).astype(F32)[:P_CHUNK]
    left = _lane_is_left((P_CHUNK, LANES))
    n_stack = 2 * N_QSLABS

    def attn_seq(s, carry):
        r0 = pl.multiple_of(s * P_CHUNK, P_CHUNK)
        kv_cur = qkv_sc[pl.ds(r0, P_CHUNK), ATTN_WIDTH:]
        kv_cat = jnp.concatenate([kvprev_sc[pl.ds(r0, P_CHUNK), :], kv_cur], axis=0)
        k_cat = kv_cat[:, :KV_WIDTH]
        v_ext = jnp.concatenate([kv_cat[:, KV_WIDTH:], jnp.ones((2 * P_CHUNK, LANES), BF16)], axis=1)
        qs = []
        for j in range(N_QSLABS):
            q = qkv_sc[pl.ds(r0, P_CHUNK), j * LANES:(j + 1) * LANES]
            qs += [jnp.where(left, q, 0), jnp.where(left, 0, q)]
        q_all = jnp.concatenate(qs, axis=0)
        sc = _dot_t(q_all, k_cat).reshape(n_stack, P_CHUNK, 2 * P_CHUNK) + bias[None]
        sc = sc.reshape(n_stack * P_CHUNK, 2 * P_CHUNK)
        sink = sink_ref[...]
        m = jnp.max(jnp.maximum(sc[:, :LANES], sc[:, LANES:]), axis=-1, keepdims=True)
        m = jnp.maximum(jnp.broadcast_to(m, sink.shape), sink)
        p = jnp.concatenate([jnp.exp(sc[:, :LANES] - m), jnp.exp(sc[:, LANES:] - m)], axis=1)
        o_ext = _dot(p.astype(BF16), v_ext)
        o_all = o_ext[:, :LANES] / (o_ext[:, LANES:] + jnp.exp(sink - m))
        for j in range(N_QSLABS):
            o_sc[pl.ds(r0, P_CHUNK), j * LANES:(j + 1) * LANES] = jnp.where(
                left, o_all[2 * j * P_CHUNK:(2 * j + 1) * P_CHUNK],
                o_all[(2 * j + 1) * P_CHUNK:(2 * j + 2) * P_CHUNK])
        return carry

    lax.fori_loop(0, P_SEQ, attn_seq, 0, unroll=2)
    kvprev_sc[...] = qkv_sc[:, ATTN_WIDTH:]

    sub_rows = SCAN_SUB * P_SEQ

    def s5_sub(c, carry):
        row0 = pl.multiple_of(c * sub_rows, sub_rows)
        _s5_block(u_tm, row0, sub_rows, bu_sc, h_ref, lam_r_ref, lam_i_ref, bcat_ref, ccat_ref,
                  dskip_ref, SCAN_SUB, P_SEQ, 4)
        return carry

    lax.fori_loop(0, P_CHUNK // SCAN_SUB, s5_sub, 0)
    y_ssm = jnp.concatenate([u_tm[k] for k in range(N_PARTS)], axis=1)
    ssm_n = _glu_norm(y_ssm, w_glu_ref, b_glu_ref, g_ssm_ref)
    mix_tm = _dot(ssm_n.astype(BF16), wo_s_ref[...])

    attn_n = _rms(o_sc[...], g_attn_ref[...])
    x1_bm = load_x() + _dot(attn_n.astype(BF16), wo_a_ref[...])
    for s in range(P_SEQ):
        for j in range(D_MODEL // LANES):
            xt_sc[j, pl.ds(s, P_CHUNK, stride=P_SEQ), :] = (
                x1_bm[s * P_CHUNK:(s + 1) * P_CHUNK, j * LANES:(j + 1) * LANES])
    x1_ref[...] = jnp.concatenate([xt_sc[j] for j in range(D_MODEL // LANES)], axis=1) + mix_tm


def _ffn_kernel(x1_ref, conv0_ref, g_ffn_ref, w_gate_ref, w_up_ref, conv_w_ref, conv_b_ref,
                w_down_ref, g_final_ref, y_ref, conv_ref, h_sc, *rest, rows_per_step, to_batch_major):
    i = pl.program_id(0)
    n_rows = x1_ref.shape[0]
    halo = 2 * rows_per_step

    @pl.when(i == 0)
    def _():
        conv_ref[...] = conv0_ref[...]

    x1 = x1_ref[...]
    xn = _rms(x1, g_ffn_ref[...]).astype(BF16)
    for sl in range(N_FFN_SLICES):
        c0, c1 = sl * FFN_SLICE, (sl + 1) * FFN_SLICE
        g = _dot(xn, w_gate_ref[:, c0:c1])
        up = _dot(xn, w_up_ref[:, c0:c1])
        gp = jnp.concatenate([conv_ref[:, c0:c1], g], axis=0)
        conv = (conv_b_ref[:, c0:c1]
                + conv_w_ref[0:1, c0:c1] * gp[0:n_rows]
                + conv_w_ref[1:2, c0:c1] * gp[rows_per_step:rows_per_step + n_rows]
                + conv_w_ref[2:3, c0:c1] * gp[halo:halo + n_rows])
        conv_ref[:, c0:c1] = gp[n_rows:n_rows + halo]
        h_sc[:, c0:c1] = (jax.nn.silu(conv) * up).astype(BF16)
    x2 = x1 + _dot(h_sc[...], w_down_ref[...])
    y = _rms(x2, g_final_ref[...])
    if to_batch_major:
        (yt_sc,) = rest
        n_seq = rows_per_step
        n_t = n_rows // n_seq
        for j in range(D_MODEL // LANES):
            yt_sc[j] = y[:, j * LANES:(j + 1) * LANES]
        for s in range(n_seq):
            for j in range(D_MODEL // LANES):
                y_ref[s, :, j * LANES:(j + 1) * LANES] = yt_sc[j, pl.ds(s, n_t, stride=n_seq), :]
    else:
        y_ref[...] = y


def _sample_inproj_kernel(x_ref, h0_ref, g_mix_ref, w_in_ref, lam_r_ref, lam_i_ref, bcat_ref,
                          ccat_ref, dskip_ref, w_glu_ref, b_glu_ref, g_ssm_ref,
                          q_ref, k_ref, v_ref, ssm_ref, h_ref, u_tm, bu_sc, *, n_steps, n_seq):
    n_rows = n_steps * n_seq
    xn = _rms(x_ref[...], g_mix_ref[...]).astype(BF16)
    qkv = _dot(xn, w_in_ref[:, :QKV_COLS])
    q_ref[...] = qkv[:, :ATTN_WIDTH].astype(BF16)
    k_ref[...] = qkv[:, ATTN_WIDTH:ATTN_WIDTH + KV_WIDTH]
    v_ref[...] = qkv[:, ATTN_WIDTH + KV_WIDTH:]
    u = _dot(xn, w_in_ref[:, QKV_COLS:])
    for k in range(N_PARTS):
        u_tm[k] = u[:, k * LANES:(k + 1) * LANES]
    h_ref[...] = h0_ref[...]
    _s5_block(u_tm, 0, n_rows, bu_sc, h_ref, lam_r_ref, lam_i_ref, bcat_ref, ccat_ref,
              dskip_ref, n_steps, n_seq, True)
    y_ssm = jnp.concatenate([u_tm[k] for k in range(N_PARTS)], axis=1)
    ssm_ref[...] = _glu_norm(y_ssm, w_glu_ref, b_glu_ref, g_ssm_ref).astype(BF16)


def _sample_attn_kernel(q_ref, kn_ref, vn_ref, kc_ref, vc_ref, sink_ref, x_ref, ssm_ref,
                        g_attn_ref, wo_a_ref, wo_s_ref, x1_ref, o_sc, bias_c_sc, bias_n_sc,
                        *, n_steps, n_seq, seq_batch, n_cache):
    b = pl.program_id(0)
    rows = n_steps * seq_batch
    q_rows = N_QSLABS * rows

    @pl.when(b == 0)
    def _():
        r = lax.broadcasted_iota(jnp.int32, (q_rows, seq_batch * n_cache), 0)
        c = lax.broadcasted_iota(jnp.int32, (q_rows, seq_batch * n_cache), 1)
        t = (r % rows) // seq_batch
        ok = ((r % seq_batch) == (c // n_cache)) & ((c % n_cache) > t)
        bias_c_sc[...] = jnp.where(ok, 0.0, -jnp.inf).astype(F32)
        r = lax.broadcasted_iota(jnp.int32, (q_rows, rows), 0)
        c = lax.broadcasted_iota(jnp.int32, (q_rows, rows), 1)
        t = (r % rows) // seq_batch
        ok = ((r % seq_batch) == (c % seq_batch)) & ((c // seq_batch) <= t)
        bias_n_sc[...] = jnp.where(ok, 0.0, -jnp.inf).astype(F32)

    q = q_ref[...].reshape(rows, ATTN_WIDTH)
    kc = kc_ref[...].astype(BF16)
    vc = vc_ref[...].astype(BF16)
    kn = kn_ref[...].reshape(rows, KV_WIDTH).astype(BF16)
    vn = vn_ref[...].reshape(rows, KV_WIDTH).astype(BF16)
    left = _lane_is_left((rows, LANES))
    halves = []
    for grp in range(N_KV_HEADS):
        qs = []
        for j in range(N_QSLABS):
            qj = q[:, j * LANES:(j + 1) * LANES]
            qs.append(jnp.where(left, qj, 0) if grp == 0 else jnp.where(left, 0, qj))
        qg = jnp.concatenate(qs, axis=0)
        sc_c = _dot_t(qg, kc) + bias_c_sc[...]
        sc_n = _dot_t(qg, kn) + bias_n_sc[...]
        (p_c, p_n), denom = _softmax_sink([sc_c, sc_n], sink_ref[grp])
        halves.append((_dot(p_c.astype(BF16), vc) + _dot(p_n.astype(BF16), vn)) / denom)
    for j in range(N_QSLABS):
        o_j = jnp.where(left, halves[0][j * rows:(j + 1) * rows], halves[1][j * rows:(j + 1) * rows])
        s0 = pl.multiple_of(b * seq_batch, seq_batch)
        o_sc[:, pl.ds(s0, seq_batch), j * LANES:(j + 1) * LANES] = (
            o_j.reshape(n_steps, seq_batch, LANES))

    @pl.when(b == pl.num_programs(0) - 1)
    def _():
        attn_n = _rms(o_sc[...].reshape(n_steps * n_seq, ATTN_WIDTH), g_attn_ref[...])
        x1_ref[...] = (x_ref[...] + _dot(attn_n.astype(BF16), wo_a_ref[...])
                       + _dot(ssm_ref[...], wo_s_ref[...]))


def _full(shape):
    return pl.BlockSpec(shape, lambda *_: (0,) * len(shape))


def _const(shape):
    return pl.BlockSpec(shape, lambda *_: (0,) * len(shape), pipeline_mode=pl.Buffered(1))


def _params(semantics):
    return pltpu.CompilerParams(dimension_semantics=semantics, vmem_limit_bytes=VMEM_LIMIT_BYTES)


def _ffn_call(x1_tm, conv0, w, *, rows_per_step, block_rows, to_batch_major):
    n_rows = x1_tm.shape[0]
    n_blocks = n_rows // block_rows
    halo = 2 * rows_per_step
    in_specs = [
        pl.BlockSpec((block_rows, D_MODEL), lambda i: (i, 0)),
        _const((halo, D_FF)), _const((1, D_MODEL)), _const((D_MODEL, D_FF)), _const((D_MODEL, D_FF)),
        _const((3, D_FF)), _const((1, D_FF)), _const((D_FF, D_MODEL)), _const((1, D_MODEL)),
    ]
    scratch = [pltpu.VMEM((block_rows, D_FF), BF16)]
    if to_batch_major:
        n_t = block_rows // rows_per_step
        y_shape = jax.ShapeDtypeStruct((rows_per_step, (n_blocks - 1) * n_t, D_MODEL), F32)
        y_spec = pl.BlockSpec((rows_per_step, n_t, D_MODEL), lambda i: (0, jnp.maximum(i - 1, 0), 0))
        scratch.append(pltpu.VMEM((D_MODEL // LANES, block_rows, LANES), F32))
    else:
        y_shape = jax.ShapeDtypeStruct((n_rows, D_MODEL), F32)
        y_spec = pl.BlockSpec((block_rows, D_MODEL), lambda i: (i, 0))
    return pl.pallas_call(
        functools.partial(_ffn_kernel, rows_per_step=rows_per_step, to_batch_major=to_batch_major),
        grid=(n_blocks,),
        in_specs=in_specs,
        out_specs=[y_spec, _full((halo, D_FF))],
        out_shape=[y_shape, jax.ShapeDtypeStruct((halo, D_FF), F32)],
        scratch_shapes=scratch,
        compiler_params=_params(("arbitrary",)),
        name="conv_ffn",
    )(x1_tm, conv0, w["g_ffn"], w["w_gate"], w["w_up"], w["conv_w"], w["conv_b"], w["w_down"],
      w["g_final"])


def _prep_weights(g_mix, w_in, sinks, lam_re, lam_im, log_dt, b_re, b_im, c_re, c_im, d_skip,
                  w_glu, b_glu, g_attn_out, g_ssm_out, w_o, g_ffn, w_gate, w_up, conv_w, conv_b,
                  w_down, g_final):
    group = N_HEADS // N_KV_HEADS
    perm = jnp.concatenate([
        jnp.concatenate([jnp.arange(HEAD_DIM) + j * HEAD_DIM,
                         jnp.arange(HEAD_DIM) + (group + j) * HEAD_DIM]) for j in range(group)])
    w_q = w_in[:, :ATTN_WIDTH][:, perm] * (HEAD_DIM ** -0.5)
    w = {
        "g_mix": g_mix.reshape(1, D_MODEL),
        "w_in": jnp.concatenate([w_q, w_in[:, ATTN_WIDTH:]], axis=1).astype(BF16),
        "g_attn": g_attn_out[perm].reshape(1, ATTN_WIDTH),
        "g_ssm": g_ssm_out.reshape(1, SSM_WIDTH),
        "wo_a": w_o[:ATTN_WIDTH][perm].astype(BF16),
        "wo_s": w_o[ATTN_WIDTH:].astype(BF16),
        "w_glu": w_glu.astype(BF16),
        "b_glu": b_glu.reshape(1, SSM_WIDTH),
        "dskip": d_skip.reshape(1, SSM_WIDTH),
        "g_ffn": g_ffn.reshape(1, D_MODEL),
        "w_gate": w_gate.astype(BF16),
        "w_up": w_up.astype(BF16),
        "conv_w": conv_w,
        "conv_b": conv_b.reshape(1, D_FF),
        "w_down": w_down.astype(BF16),
        "g_final": g_final.reshape(1, D_MODEL),
    }
    lam = lax.complex(lam_re, lam_im)
    lam_bar = jnp.exp(lam * jnp.exp(log_dt)[:, None])
    b_bar = ((lam_bar - 1.0) / lam)[..., None] * lax.complex(b_re, b_im)
    w["lam_r"] = jnp.real(lam_bar).reshape(1, N_SSM_GROUPS * SSM_STATE)
    w["lam_i"] = jnp.imag(lam_bar).reshape(1, N_SSM_GROUPS * SSM_STATE)
    gpp = LANES // SSM_GROUP
    eye = jnp.eye(gpp, dtype=F32)

    def b_part(b):
        b = b.reshape(N_PARTS, gpp, SSM_STATE, SSM_GROUP)
        return jnp.einsum("kgpc,gh->kgchp", b, eye).reshape(N_PARTS, LANES, PART_STATE)

    def c_part(c):
        c = c.reshape(N_PARTS, gpp, SSM_GROUP, SSM_STATE)
        return jnp.einsum("kgcp,gh->kgphc", c, eye).reshape(N_PARTS, PART_STATE, LANES)

    w["bcat"] = jnp.concatenate([b_part(jnp.real(b_bar)), b_part(jnp.imag(b_bar))], axis=2).astype(BF16)
    w["ccat"] = jnp.concatenate([c_part(c_re), -c_part(c_im)], axis=1).astype(BF16)
    sink_rows = jnp.repeat(sinks.reshape(N_KV_HEADS, group).T.reshape(N_HEADS), P_CHUNK)
    w["sink_prompt"] = jnp.broadcast_to(sink_rows[:, None], (N_HEADS * P_CHUNK, LANES))
    w["sinks"] = sinks
    return w


def _state_to_cols(re, im):
    n = re.shape[0]
    st = jnp.stack([re.reshape(n, N_PARTS, PART_STATE), im.reshape(n, N_PARTS, PART_STATE)], axis=2)
    return st.reshape(n, STATE_COLS)


def _cols_to_state(h):
    n = h.shape[0]
    st = h.reshape(n, N_PARTS, 2, PART_STATE)
    re = st[:, :, 0].reshape(n, N_SSM_GROUPS, SSM_STATE)
    im = st[:, :, 1].reshape(n, N_SSM_GROUPS, SSM_STATE)
    return re, im


def _prompt_path(x_prompt, meta_tokens, w):
    n_seq, seq_len, _ = x_prompt.shape
    assert n_seq == P_SEQ and seq_len % P_CHUNK == 0
    n_chunks = seq_len // P_CHUNK + 1
    ins = [x_prompt, meta_tokens, w["g_mix"], w["w_in"], w["sink_prompt"], w["lam_r"], w["lam_i"],
           w["bcat"], w["ccat"], w["dskip"], w["w_glu"], w["b_glu"], w["g_attn"], w["g_ssm"],
           w["wo_a"], w["wo_s"]]
    in_specs = [pl.BlockSpec((P_SEQ, P_CHUNK, D_MODEL), lambda i: (0, jnp.maximum(i - 1, 0), 0))]
    in_specs += [_const(a.shape) for a in ins[1:]]
    x1_tm, k_last, v_last, h_last = pl.pallas_call(
        _prompt_mixer_kernel,
        grid=(n_chunks,),
        in_specs=in_specs,
        out_specs=[pl.BlockSpec((P_ROWS, D_MODEL), lambda i: (i, 0)),
                   _full((P_SEQ, P_CHUNK, KV_WIDTH)), _full((P_SEQ, P_CHUNK, KV_WIDTH)),
                   _full((P_SEQ, STATE_COLS))],
        out_shape=[jax.ShapeDtypeStruct((n_chunks * P_ROWS, D_MODEL), F32),
                   jax.ShapeDtypeStruct((P_SEQ, P_CHUNK, KV_WIDTH), F32),
                   jax.ShapeDtypeStruct((P_SEQ, P_CHUNK, KV_WIDTH), F32),
                   jax.ShapeDtypeStruct((P_SEQ, STATE_COLS), F32)],
        scratch_shapes=[
            pltpu.VMEM((P_ROWS, QKV_COLS), BF16),
            pltpu.VMEM((P_ROWS, 2 * KV_WIDTH), BF16),
            pltpu.VMEM((N_PARTS, P_ROWS, LANES), F32),
            pltpu.VMEM((SCAN_SUB * P_SEQ, STATE_COLS), F32),
            pltpu.VMEM((P_ROWS, ATTN_WIDTH), F32),
            pltpu.VMEM((D_MODEL // LANES, P_ROWS, LANES), F32),
        ],
        compiler_params=_params(("arbitrary",)),
        name="prompt_mixer",
    )(*ins)
    conv0 = jnp.zeros((2 * P_SEQ, D_FF), F32)
    y_prompt, conv_tm = _ffn_call(x1_tm, conv0, w, rows_per_step=P_SEQ, block_rows=P_ROWS,
                                  to_batch_major=True)
    p_re, p_im = _cols_to_state(h_last)
    p_conv = conv_tm.reshape(2, P_SEQ, D_FF).transpose(1, 0, 2)
    kv_shape = (1, P_SEQ, WINDOW, N_KV_HEADS, HEAD_DIM)
    return (y_prompt, k_last.reshape(kv_shape), v_last.reshape(kv_shape), p_re[None], p_im[None],
            p_conv[None])


def _sample_path(x_sample, cache_k, cache_v, st_re, st_im, st_conv, w):
    n_seq, n_steps, _ = x_sample.shape
    n_cache = cache_k.shape[1]
    n_rows = n_steps * n_seq
    seq_batch = 16
    x_tm = x_sample.transpose(1, 0, 2).reshape(n_rows, D_MODEL)
    h0 = _state_to_cols(st_re, st_im)
    ins = [x_tm, h0, w["g_mix"], w["w_in"], w["lam_r"], w["lam_i"], w["bcat"], w["ccat"],
           w["dskip"], w["w_glu"], w["b_glu"], w["g_ssm"]]
    q, k_new, v_new, ssm_n, h_last = pl.pallas_call(
        functools.partial(_sample_inproj_kernel, n_steps=n_steps, n_seq=n_seq),
        grid=(1,),
        in_specs=[_full(a.shape) for a in ins],
        out_specs=[_full((n_rows, ATTN_WIDTH)), _full((n_rows, KV_WIDTH)), _full((n_rows, KV_WIDTH)),
                   _full((n_rows, SSM_WIDTH)), _full((n_seq, STATE_COLS))],
        out_shape=[jax.ShapeDtypeStruct((n_rows, ATTN_WIDTH), BF16),
                   jax.ShapeDtypeStruct((n_rows, KV_WIDTH), F32),
                   jax.ShapeDtypeStruct((n_rows, KV_WIDTH), F32),
                   jax.ShapeDtypeStruct((n_rows, SSM_WIDTH), BF16),
                   jax.ShapeDtypeStruct((n_seq, STATE_COLS), F32)],
        scratch_shapes=[pltpu.VMEM((N_PARTS, n_rows, LANES), F32),
                        pltpu.VMEM((n_rows, STATE_COLS), F32)],
        compiler_params=_params(("arbitrary",)),
        name="sample_inproj",
    )(*ins)

    group = N_HEADS // N_KV_HEADS
    rows = n_steps * seq_batch
    sink_sample = jnp.stack([
        jnp.concatenate([jnp.full((rows, 1), 1.0) * w["sinks"][g * group + j] for j in range(group)])
        for g in range(N_KV_HEADS)]).astype(F32)
    kc = cache_k.reshape(n_seq * n_cache, KV_WIDTH)
    vc = cache_v.reshape(n_seq * n_cache, KV_WIDTH)
    step_block = lambda width: pl.BlockSpec((n_steps, seq_batch, width), lambda b: (0, b, 0))
    cache_block = pl.BlockSpec((seq_batch * n_cache, KV_WIDTH), lambda b: (b, 0))
    ins = [q.reshape(n_steps, n_seq, ATTN_WIDTH), k_new.reshape(n_steps, n_seq, KV_WIDTH),
           v_new.reshape(n_steps, n_seq, KV_WIDTH), kc, vc, sink_sample, x_tm, ssm_n,
           w["g_attn"], w["wo_a"], w["wo_s"]]
    in_specs = [step_block(ATTN_WIDTH), step_block(KV_WIDTH), step_block(KV_WIDTH), cache_block,
                cache_block] + [_const(a.shape) for a in ins[5:]]
    x1_tm = pl.pallas_call(
        functools.partial(_sample_attn_kernel, n_steps=n_steps, n_seq=n_seq, seq_batch=seq_batch,
                          n_cache=n_cache),
        grid=(n_seq // seq_batch,),
        in_specs=in_specs,
        out_specs=_full((n_rows, D_MODEL)),
        out_shape=jax.ShapeDtypeStruct((n_rows, D_MODEL), F32),
        scratch_shapes=[pltpu.VMEM((n_steps, n_seq, ATTN_WIDTH), F32),
                        pltpu.VMEM((N_QSLABS * rows, seq_batch * n_cache), F32),
                        pltpu.VMEM((N_QSLABS * rows, rows), F32)],
        compiler_params=_params(("arbitrary",)),
        name="sample_attn",
    )(*ins)

    conv0 = st_conv.transpose(1, 0, 2).reshape(2 * n_seq, D_FF)
    y_tm, conv_tm = _ffn_call(x1_tm, conv0, w, rows_per_step=n_seq, block_rows=n_rows,
                              to_batch_major=False)
    y_sample = y_tm.reshape(n_steps, n_seq, D_MODEL).transpose(1, 0, 2)
    s_conv = conv_tm.reshape(2, n_seq, D_FF).transpose(1, 0, 2)
    s_re, s_im = _cols_to_state(h_last)
    kv_shape = (n_seq, n_steps, N_KV_HEADS, HEAD_DIM)
    k_new = k_new.reshape(n_steps, n_seq, KV_WIDTH).transpose(1, 0, 2).reshape(kv_shape)
    v_new = v_new.reshape(n_steps, n_seq, KV_WIDTH).transpose(1, 0, 2).reshape(kv_shape)
    s_k = jnp.concatenate([cache_k[:, n_steps:], k_new], axis=1)
    s_v = jnp.concatenate([cache_v[:, n_steps:], v_new], axis=1)
    return y_sample, s_k[None], s_v[None], s_re[None], s_im[None], s_conv[None]


def kernel(x_prompt, x_sample, cache_k_win, cache_v_win, state_ssm_re, state_ssm_im, state_conv, meta_tokens, g_mix, w_in, sinks, lam_re, lam_im, log_dt, b_re, b_im, c_re, c_im, d_skip, w_glu, b_glu, g_attn_out, g_ssm_out, w_o, g_ffn, w_gate, w_up, conv_w, conv_b, w_down, g_final):
    assert g_mix.shape[0] == 1, "single-layer kernel"
    w = _prep_weights(g_mix[0], w_in[0], sinks[0], lam_re[0], lam_im[0], log_dt[0], b_re[0], b_im[0],
                      c_re[0], c_im[0], d_skip[0], w_glu[0], b_glu[0], g_attn_out[0], g_ssm_out[0],
                      w_o[0], g_ffn[0], w_gate[0], w_up[0], conv_w[0], conv_b[0], w_down[0], g_final)
    y_p, p_k, p_v, p_re, p_im, p_conv = _prompt_path(x_prompt, meta_tokens, w)
    y_s, s_k, s_v, s_re, s_im, s_conv = _sample_path(
        x_sample, cache_k_win[0], cache_v_win[0], state_ssm_re[0], state_ssm_im[0], state_conv[0], w)
    return (y_p, y_s, p_k, p_v, p_re, p_im, p_conv, s_k, s_v, s_re, s_im, s_conv)
```

```python
import functools
import math

import jax
import jax.numpy as jnp
from jax import lax
from jax.experimental import pallas as pl
from jax.experimental.pallas import tpu as pltpu

F32 = jnp.float32
BF16 = jnp.bfloat16

D_MODEL = 1024
N_META = 16
HEAD_DIM = 64
ATTN_WIDTH = 512
N_HEADS = 8
N_KV_HEADS = 2
KV_WIDTH = 128
WINDOW = 128
SSM_WIDTH = 512
SSM_GROUP = 16
N_SSM_GROUPS = 32
SSM_STATE = 64
D_FF = 2816
EPS = 1e-5

LANES = 128
SUBLANES = 8
N_QSLABS = ATTN_WIDTH // LANES
N_PARTS = SSM_WIDTH // LANES
PART_STATE = (LANES // SSM_GROUP) * SSM_STATE
STATE_COLS = N_PARTS * 2 * PART_STATE
FFN_SLICE = 256
N_FFN_SLICES = D_FF // FFN_SLICE
QKV_COLS = ATTN_WIDTH + 2 * KV_WIDTH
VMEM_LIMIT_BYTES = 58 * 1024 * 1024

P_CHUNK = 128
P_SEQ = 8
P_ROWS = P_CHUNK * P_SEQ
P_FRONT = P_CHUNK - N_META
SCAN_SUB = 32


def _rms(x, g):
    return x * lax.rsqrt(jnp.mean(x * x, axis=-1, keepdims=True) + EPS) * g


def _dot(a, b):
    return jnp.dot(a, b, preferred_element_type=F32)


def _dot_t(a, b):
    return lax.dot_general(a, b, (((1,), (1,)), ((), ())), preferred_element_type=F32)


def _lane_is_left(shape):
    return lax.broadcasted_iota(jnp.int32, shape, len(shape) - 1) < HEAD_DIM


def _s5_scan(bu_ref, h_ref, lam_r_ref, lam_i_ref, n_steps, rows_per_step, unroll):
    n_tiles = rows_per_step // SUBLANES

    def tile_body(r, carry):
        r0 = pl.multiple_of(r * SUBLANES, SUBLANES)
        for k in range(N_PARTS):
            c_re = k * 2 * PART_STATE
            c_im = c_re + PART_STATE
            lr = jnp.broadcast_to(lam_r_ref[:, k * PART_STATE:(k + 1) * PART_STATE],
                                  (SUBLANES, PART_STATE))
            li = jnp.broadcast_to(lam_i_ref[:, k * PART_STATE:(k + 1) * PART_STATE],
                                  (SUBLANES, PART_STATE))
            hr = h_ref[pl.ds(r0, SUBLANES), c_re:c_re + PART_STATE]
            hi = h_ref[pl.ds(r0, SUBLANES), c_im:c_im + PART_STATE]

            def step(t, h):
                hr, hi = h
                row = pl.multiple_of(t * rows_per_step + r0, SUBLANES)
                br = bu_ref[pl.ds(row, SUBLANES), c_re:c_re + PART_STATE]
                bi = bu_ref[pl.ds(row, SUBLANES), c_im:c_im + PART_STATE]
                nr = lr * hr - li * hi + br
                ni = lr * hi + li * hr + bi
                bu_ref[pl.ds(row, SUBLANES), c_re:c_re + PART_STATE] = nr
                bu_ref[pl.ds(row, SUBLANES), c_im:c_im + PART_STATE] = ni
                return nr, ni

            hr, hi = lax.fori_loop(0, n_steps, step, (hr, hi), unroll=unroll)
            h_ref[pl.ds(r0, SUBLANES), c_re:c_re + PART_STATE] = hr
            h_ref[pl.ds(r0, SUBLANES), c_im:c_im + PART_STATE] = hi
        return carry

    if n_tiles == 1:
        tile_body(0, 0)
    else:
        lax.fori_loop(0, n_tiles, tile_body, 0)


def _s5_block(u_tm_ref, row0, n_rows, bu_ref, h_ref, lam_r_ref, lam_i_ref,
              bcat_ref, ccat_ref, dskip_ref, n_steps, rows_per_step, unroll):
    for k in range(N_PARTS):
        u_k = u_tm_ref[k, pl.ds(row0, n_rows), :].astype(BF16)
        bu_ref[:, k * 2 * PART_STATE:(k + 1) * 2 * PART_STATE] = _dot(u_k, bcat_ref[k])
    _s5_scan(bu_ref, h_ref, lam_r_ref, lam_i_ref, n_steps, rows_per_step, unroll)
    for k in range(N_PARTS):
        h_k = bu_ref[:, k * 2 * PART_STATE:(k + 1) * 2 * PART_STATE].astype(BF16)
        y_k = _dot(h_k, ccat_ref[k])
        u_k = u_tm_ref[k, pl.ds(row0, n_rows), :]
        u_tm_ref[k, pl.ds(row0, n_rows), :] = y_k + dskip_ref[:, k * LANES:(k + 1) * LANES] * u_k


def _glu_norm(y, w_glu_ref, b_glu_ref, g_ssm_ref):
    g = jax.nn.gelu(y)
    gate = jax.nn.sigmoid(_dot(g.astype(BF16), w_glu_ref[...]) + b_glu_ref[...])
    return _rms(g * gate, g_ssm_ref[...])


def _softmax_sink(parts, sink):
    m = sink
    for s in parts:
        m = jnp.maximum(m, jnp.max(s, axis=-1, keepdims=True))
    ps = [jnp.exp(s - m) for s in parts]
    denom = jnp.exp(sink - m)
    for p in ps:
        denom = denom + jnp.sum(p, axis=-1, keepdims=True)
    return ps, denom


def _prompt_mixer_kernel(x_ref, meta_ref, g_mix_ref, w_in_ref, sink_ref, lam_r_ref, lam_i_ref,
                         bcat_ref, ccat_ref, dskip_ref, w_glu_ref, b_glu_ref, g_attn_ref,
                         g_ssm_ref, wo_a_ref, wo_s_ref,
                         x1_ref, k_out_ref, v_out_ref, h_ref,
                         qkv_sc, kvprev_sc, u_tm, bu_sc, o_sc, xt_sc):
    i = pl.program_id(0)

    @pl.when(i == 0)
    def _():
        h_ref[...] = jnp.zeros_like(h_ref)
        kvprev_sc[...] = jnp.zeros_like(kvprev_sc)

    def load_x():
        first = jnp.concatenate([jnp.zeros((P_FRONT, D_MODEL), F32), meta_ref[...]], axis=0)
        first = jnp.broadcast_to(first[None], (P_SEQ, P_CHUNK, D_MODEL))
        return jnp.where(i == 0, first, x_ref[...]).reshape(P_ROWS, D_MODEL)

    xn = _rms(load_x(), g_mix_ref[...]).astype(BF16)
    qkv = _dot(xn, w_in_ref[:, :QKV_COLS])
    k_out_ref[...] = qkv[:, ATTN_WIDTH:ATTN_WIDTH + KV_WIDTH].reshape(P_SEQ, P_CHUNK, KV_WIDTH)
    v_out_ref[...] = qkv[:, ATTN_WIDTH + KV_WIDTH:].reshape(P_SEQ, P_CHUNK, KV_WIDTH)
    qkv_sc[...] = qkv.astype(BF16)
    u = _dot(xn, w_in_ref[:, QKV_COLS:])
    for s in range(P_SEQ):
        for k in range(N_PARTS):
            u_tm[k, pl.ds(s, P_CHUNK, stride=P_SEQ), :] = (
                u[s * P_CHUNK:(s + 1) * P_CHUNK, k * LANES:(k + 1) * LANES])

    tq = lax.broadcasted_iota(jnp.int32, (2 * P_CHUNK, 2 * P_CHUNK), 0) % P_CHUNK
    col = lax.broadcasted_iota(jnp.int32, (2 * P_CHUNK, 2 * P_CHUNK), 1)
    is_prev = col < P_CHUNK
    tk = jnp.where(is_prev, col, col - P_CHUNK)
    kpos = jnp.where(is_prev, (i - 1) * P_CHUNK, i * P_CHUNK) + tk - P_FRONT
    valid = ((is_prev & (tk > tq)) | (jnp.logical_not(is_prev) & (tk <= tq))) & (kpos >= 0)
    bias = jnp.where(valid, 0.0, -jnp.inf).astype(F32)[:P_CHUNK]
    left = _lane_is_left((P_CHUNK, LANES))
    n_stack = 2 * N_QSLABS

    def attn_seq(s, carry):
        r0 = pl.multiple_of(s * P_CHUNK, P_CHUNK)
        kv_cur = qkv_sc[pl.ds(r0, P_CHUNK), ATTN_WIDTH:]
        kv_cat = jnp.concatenate([kvprev_sc[pl.ds(r0, P_CHUNK), :], kv_cur], axis=0)
        k_cat = kv_cat[:, :KV_WIDTH]
        v_ext = jnp.concatenate([kv_cat[:, KV_WIDTH:], jnp.ones((2 * P_CHUNK, LANES), BF16)], axis=1)
        qs = []
        for j in range(N_QSLABS):
            q = qkv_sc[pl.ds(r0, P_CHUNK), j * LANES:(j + 1) * LANES]
            qs += [jnp.where(left, q, 0), jnp.where(left, 0, q)]
        q_all = jnp.concatenate(qs, axis=0)
        sc = _dot_t(q_all, k_cat).reshape(n_stack, P_CHUNK, 2 * P_CHUNK) + bias[None]
        sc = sc.reshape(n_stack * P_CHUNK, 2 * P_CHUNK)
        sink = sink_ref[...]
        m = jnp.max(jnp.maximum(sc[:, :LANES], sc[:, LANES:]), axis=-1, keepdims=True)
        m = jnp.maximum(jnp.broadcast_to(m, sink.shape), sink)
        p = jnp.concatenate([jnp.exp(sc[:, :LANES] - m), jnp.exp(sc[:, LANES:] - m)], axis=1)
        o_ext = _dot(p.astype(BF16), v_ext)
        o_all = o_ext[:, :LANES] / (o_ext[:, LANES:] + jnp.exp(sink - m))
        for j in range(N_QSLABS):
            o_sc[pl.ds(r0, P_CHUNK), j * LANES:(j + 1) * LANES] = jnp.where(
                left, o_all[2 * j * P_CHUNK:(2 * j + 1) * P_CHUNK],
                o_all[(2 * j + 1) * P_CHUNK:(2 * j + 2) * P_CHUNK])
        return carry

    lax.fori_loop(0, P_SEQ, attn_seq, 0, unroll=2)
    kvprev_sc[...] = qkv_sc[:, ATTN_WIDTH:]

    sub_rows = SCAN_SUB * P_SEQ

    def s5_sub(c, carry):
        row0 = pl.multiple_of(c * sub_rows, sub_rows)
        _s5_block(u_tm, row0, sub_rows, bu_sc, h_ref, lam_r_ref, lam_i_ref, bcat_ref, ccat_ref,
                  dskip_ref, SCAN_SUB, P_SEQ, 4)
        return carry

    lax.fori_loop(0, P_CHUNK // SCAN_SUB, s5_sub, 0)
    y_ssm = jnp.concatenate([u_tm[k] for k in range(N_PARTS)], axis=1)
    ssm_n = _glu_norm(y_ssm, w_glu_ref, b_glu_ref, g_ssm_ref)
    mix_tm = _dot(ssm_n.astype(BF16), wo_s_ref[...])

    attn_n = _rms(o_sc[...], g_attn_ref[...])
    x1_bm = load_x() + _dot(attn_n.astype(BF16), wo_a_ref[...])
    for s in range(P_SEQ):
        for j in range(D_MODEL // LANES):
            xt_sc[j, pl.ds(s, P_CHUNK, stride=P_SEQ), :] = (
                x1_bm[s * P_CHUNK:(s + 1) * P_CHUNK, j * LANES:(j + 1) * LANES])
    x1_ref[...] = jnp.concatenate([xt_sc[j] for j in range(D_MODEL // LANES)], axis=1) + mix_tm


def _ffn_kernel(x1_ref, conv0_ref, g_ffn_ref, w_gate_ref, w_up_ref, conv_w_ref, conv_b_ref,
                w_down_ref, g_final_ref, y_ref, conv_ref, h_sc, *rest, rows_per_step, to_batch_major):
    i = pl.program_id(0)
    n_rows = x1_ref.shape[0]
    halo = 2 * rows_per_step

    @pl.when(i == 0)
    def _():
        conv_ref[...] = conv0_ref[...]

    x1 = x1_ref[...]
    xn = _rms(x1, g_ffn_ref[...]).astype(BF16)
    for sl in range(N_FFN_SLICES):
        c0, c1 = sl * FFN_SLICE, (sl + 1) * FFN_SLICE
        g = _dot(xn, w_gate_ref[:, c0:c1])
        up = _dot(xn, w_up_ref[:, c0:c1])
        gp = jnp.concatenate([conv_ref[:, c0:c1], g], axis=0)
        conv = (conv_b_ref[:, c0:c1]
                + conv_w_ref[0:1, c0:c1] * gp[0:n_rows]
                + conv_w_ref[1:2, c0:c1] * gp[rows_per_step:rows_per_step + n_rows]
                + conv_w_ref[2:3, c0:c1] * gp[halo:halo + n_rows])
        conv_ref[:, c0:c1] = gp[n_rows:n_rows + halo]
        h_sc[:, c0:c1] = (jax.nn.silu(conv) * up).astype(BF16)
    x2 = x1 + _dot(h_sc[...], w_down_ref[...])
    y = _rms(x2, g_final_ref[...])
    if to_batch_major:
        (yt_sc,) = rest
        n_seq = rows_per_step
        n_t = n_rows // n_seq
        for j in range(D_MODEL // LANES):
            yt_sc[j] = y[:, j * LANES:(j + 1) * LANES]
        for s in range(n_seq):
            for j in range(D_MODEL // LANES):
                y_ref[s, :, j * LANES:(j + 1) * LANES] = yt_sc[j, pl.ds(s, n_t, stride=n_seq), :]
    else:
        y_ref[...] = y


def _sample_inproj_kernel(x_ref, h0_re_ref, h0_im_ref, g_mix_ref, w_in_ref, lam_r_ref, lam_i_ref,
                          bcat_ref, ccat_ref, dskip_ref, w_glu_ref, b_glu_ref, g_ssm_ref,
                          q_ref, k_ref, v_ref, ssm_ref, h_re_ref, h_im_ref, u_tm, bu_sc, h_sc,
                          *, n_steps, n_seq):
    n_rows = n_steps * n_seq
    xn = _rms(x_ref[...], g_mix_ref[...]).astype(BF16)
    qkv = _dot(xn, w_in_ref[:, :QKV_COLS])
    q_ref[...] = qkv[:, :ATTN_WIDTH].astype(BF16)
    k_ref[...] = qkv[:, ATTN_WIDTH:ATTN_WIDTH + KV_WIDTH]
    v_ref[...] = qkv[:, ATTN_WIDTH + KV_WIDTH:]
    u = _dot(xn, w_in_ref[:, QKV_COLS:])
    for k in range(N_PARTS):
        u_tm[k] = u[:, k * LANES:(k + 1) * LANES]
    h0_re = h0_re_ref[...].T
    h0_im = h0_im_ref[...].T
    for k in range(N_PARTS):
        h_sc[:, k * 2 * PART_STATE:k * 2 * PART_STATE + PART_STATE] = (
            h0_re[:, k * PART_STATE:(k + 1) * PART_STATE])
        h_sc[:, k * 2 * PART_STATE + PART_STATE:(k + 1) * 2 * PART_STATE] = (
            h0_im[:, k * PART_STATE:(k + 1) * PART_STATE])
    _s5_block(u_tm, 0, n_rows, bu_sc, h_sc, lam_r_ref, lam_i_ref, bcat_ref, ccat_ref,
              dskip_ref, n_steps, n_seq, True)
    h_re_ref[...] = jnp.concatenate(
        [h_sc[:, k * 2 * PART_STATE:k * 2 * PART_STATE + PART_STATE] for k in range(N_PARTS)], axis=1).T
    h_im_ref[...] = jnp.concatenate(
        [h_sc[:, k * 2 * PART_STATE + PART_STATE:(k + 1) * 2 * PART_STATE] for k in range(N_PARTS)],
        axis=1).T
    y_ssm = jnp.concatenate([u_tm[k] for k in range(N_PARTS)], axis=1)
    ssm_ref[...] = _glu_norm(y_ssm, w_glu_ref, b_glu_ref, g_ssm_ref).astype(BF16)


def _sample_attn_kernel(q_ref, kn_ref, vn_ref, kc_ref, vc_ref, sink_ref, x_ref, ssm_ref,
                        g_attn_ref, wo_a_ref, wo_s_ref, x1_ref, o_sc, bias_c_sc, bias_n_sc,
                        *, n_steps, n_seq, seq_batch, n_cache):
    b = pl.program_id(0)
    rows = n_steps * seq_batch
    q_rows = N_QSLABS * rows

    @pl.when(b == 0)
    def _():
        r = lax.broadcasted_iota(jnp.int32, (q_rows, seq_batch * n_cache), 0)
        c = lax.broadcasted_iota(jnp.int32, (q_rows, seq_batch * n_cache), 1)
        t = (r % rows) // seq_batch
        ok = ((r % seq_batch) == (c // n_cache)) & ((c % n_cache) > t)
        bias_c_sc[...] = jnp.where(ok, 0.0, -jnp.inf).astype(F32)
        r = lax.broadcasted_iota(jnp.int32, (q_rows, rows), 0)
        c = lax.broadcasted_iota(jnp.int32, (q_rows, rows), 1)
        t = (r % rows) // seq_batch
        ok = ((r % seq_batch) == (c % seq_batch)) & ((c // seq_batch) <= t)
        bias_n_sc[...] = jnp.where(ok, 0.0, -jnp.inf).astype(F32)

    q = q_ref[...].reshape(rows, ATTN_WIDTH)
    kc = kc_ref[...].astype(BF16).reshape(seq_batch, KV_WIDTH, n_cache)
    vc = vc_ref[...].astype(BF16).reshape(seq_batch, KV_WIDTH, n_cache)
    kc = jnp.concatenate([kc[s] for s in range(seq_batch)], axis=1)
    vc = jnp.concatenate([vc[s] for s in range(seq_batch)], axis=1)
    kn = kn_ref[...].reshape(rows, KV_WIDTH).astype(BF16)
    vn = vn_ref[...].reshape(rows, KV_WIDTH).astype(BF16)
    left = _lane_is_left((rows, LANES))
    halves = []
    for grp in range(N_KV_HEADS):
        qs = []
        for j in range(N_QSLABS):
            qj = q[:, j * LANES:(j + 1) * LANES]
            qs.append(jnp.where(left, qj, 0) if grp == 0 else jnp.where(left, 0, qj))
        qg = jnp.concatenate(qs, axis=0)
        sc_c = _dot(qg, kc) + bias_c_sc[...]
        sc_n = _dot_t(qg, kn) + bias_n_sc[...]
        (p_c, p_n), denom = _softmax_sink([sc_c, sc_n], sink_ref[grp])
        halves.append((_dot_t(p_c.astype(BF16), vc) + _dot(p_n.astype(BF16), vn)) / denom)
    for j in range(N_QSLABS):
        o_j = jnp.where(left, halves[0][j * rows:(j + 1) * rows], halves[1][j * rows:(j + 1) * rows])
        s0 = pl.multiple_of(b * seq_batch, seq_batch)
        o_sc[:, pl.ds(s0, seq_batch), j * LANES:(j + 1) * LANES] = (
            o_j.reshape(n_steps, seq_batch, LANES))

    @pl.when(b == pl.num_programs(0) - 1)
    def _():
        attn_n = _rms(o_sc[...].reshape(n_steps * n_seq, ATTN_WIDTH), g_attn_ref[...])
        x1_ref[...] = (x_ref[...] + _dot(attn_n.astype(BF16), wo_a_ref[...])
                       + _dot(ssm_ref[...], wo_s_ref[...]))


def _full(shape):
    return pl.BlockSpec(shape, lambda *_: (0,) * len(shape))


def _const(shape):
    return pl.BlockSpec(shape, lambda *_: (0,) * len(shape), pipeline_mode=pl.Buffered(1))


def _params(semantics):
    return pltpu.CompilerParams(dimension_semantics=semantics, vmem_limit_bytes=VMEM_LIMIT_BYTES)


def _ffn_call(x1_tm, conv0, w, *, rows_per_step, block_rows, to_batch_major):
    n_rows = x1_tm.shape[0]
    n_blocks = n_rows // block_rows
    halo = 2 * rows_per_step
    in_specs = [
        pl.BlockSpec((block_rows, D_MODEL), lambda i: (i, 0)),
        _const((halo, D_FF)), _const((1, D_MODEL)), _const((D_MODEL, D_FF)), _const((D_MODEL, D_FF)),
        _const((3, D_FF)), _const((1, D_FF)), _const((D_FF, D_MODEL)), _const((1, D_MODEL)),
    ]
    scratch = [pltpu.VMEM((block_rows, D_FF), BF16)]
    if to_batch_major:
        n_t = block_rows // rows_per_step
        y_shape = jax.ShapeDtypeStruct((rows_per_step, (n_blocks - 1) * n_t, D_MODEL), F32)
        y_spec = pl.BlockSpec((rows_per_step, n_t, D_MODEL), lambda i: (0, jnp.maximum(i - 1, 0), 0))
        scratch.append(pltpu.VMEM((D_MODEL // LANES, block_rows, LANES), F32))
    else:
        y_shape = jax.ShapeDtypeStruct((n_rows, D_MODEL), F32)
        y_spec = pl.BlockSpec((block_rows, D_MODEL), lambda i: (i, 0))
    return pl.pallas_call(
        functools.partial(_ffn_kernel, rows_per_step=rows_per_step, to_batch_major=to_batch_major),
        grid=(n_blocks,),
        in_specs=in_specs,
        out_specs=[y_spec, _full((halo, D_FF))],
        out_shape=[y_shape, jax.ShapeDtypeStruct((halo, D_FF), F32)],
        scratch_shapes=scratch,
        compiler_params=_params(("arbitrary",)),
        name="conv_ffn",
    )(x1_tm, conv0, w["g_ffn"], w["w_gate"], w["w_up"], w["conv_w"], w["conv_b"], w["w_down"],
      w["g_final"])


def _pair_heads(x, axis):
    group = N_HEADS // N_KV_HEADS
    shape = x.shape
    x = x.reshape(shape[:axis] + (N_KV_HEADS, group, HEAD_DIM) + shape[axis + 1:])
    x = jnp.swapaxes(x, axis, axis + 1)
    return x.reshape(shape)


def _prep_weights(g_mix, w_in, sinks, lam_re, lam_im, log_dt, b_re, b_im, c_re, c_im, d_skip,
                  w_glu, b_glu, g_attn_out, g_ssm_out, w_o, g_ffn, w_gate, w_up, conv_w, conv_b,
                  w_down, g_final):
    group = N_HEADS // N_KV_HEADS
    w_q = _pair_heads(w_in[:, :ATTN_WIDTH], 1) * (HEAD_DIM ** -0.5)
    w = {
        "g_mix": g_mix.reshape(1, D_MODEL),
        "w_in": jnp.concatenate([w_q, w_in[:, ATTN_WIDTH:]], axis=1).astype(BF16),
        "g_attn": _pair_heads(g_attn_out, 0).reshape(1, ATTN_WIDTH),
        "g_ssm": g_ssm_out.reshape(1, SSM_WIDTH),
        "wo_a": _pair_heads(w_o[:ATTN_WIDTH], 0).astype(BF16),
        "wo_s": w_o[ATTN_WIDTH:].astype(BF16),
        "w_glu": w_glu.astype(BF16),
        "b_glu": b_glu.reshape(1, SSM_WIDTH),
        "dskip": d_skip.reshape(1, SSM_WIDTH),
        "g_ffn": g_ffn.reshape(1, D_MODEL),
        "w_gate": w_gate.astype(BF16),
        "w_up": w_up.astype(BF16),
        "conv_w": conv_w,
        "conv_b": conv_b.reshape(1, D_FF),
        "w_down": w_down.astype(BF16),
        "g_final": g_final.reshape(1, D_MODEL),
    }
    dt = jnp.exp(log_dt)[:, None]
    mag = jnp.exp(lam_re * dt)
    lb_r = mag * jnp.cos(lam_im * dt)
    lb_i = mag * jnp.sin(lam_im * dt)
    inv = 1.0 / (lam_re * lam_re + lam_im * lam_im)
    cf_r = ((lb_r - 1.0) * lam_re + lb_i * lam_im) * inv
    cf_i = (lb_i * lam_re - (lb_r - 1.0) * lam_im) * inv
    bb_r = cf_r[..., None] * b_re - cf_i[..., None] * b_im
    bb_i = cf_r[..., None] * b_im + cf_i[..., None] * b_re
    w["lam_r"] = lb_r.reshape(1, N_SSM_GROUPS * SSM_STATE)
    w["lam_i"] = lb_i.reshape(1, N_SSM_GROUPS * SSM_STATE)
    gpp = LANES // SSM_GROUP
    eye = jnp.eye(gpp, dtype=F32)

    def b_part(b):
        b = b.reshape(N_PARTS, gpp, SSM_STATE, SSM_GROUP).transpose(0, 1, 3, 2)
        b = b[:, :, :, None, :] * eye[None, :, None, :, None]
        return b.reshape(N_PARTS, LANES, PART_STATE)

    def c_part(c):
        c = c.reshape(N_PARTS, gpp, SSM_GROUP, SSM_STATE).transpose(0, 1, 3, 2)
        c = c[:, :, :, None, :] * eye[None, :, None, :, None]
        return c.reshape(N_PARTS, PART_STATE, LANES)

    w["bcat"] = jnp.concatenate([b_part(bb_r), b_part(bb_i)], axis=2).astype(BF16)
    w["ccat"] = jnp.concatenate([c_part(c_re), -c_part(c_im)], axis=1).astype(BF16)
    sink_rows = jnp.repeat(sinks.reshape(N_KV_HEADS, group).T.reshape(N_HEADS), P_CHUNK)
    w["sink_prompt"] = jnp.broadcast_to(sink_rows[:, None], (N_HEADS * P_CHUNK, LANES))
    w["sinks"] = sinks
    return w


def _cols_to_state(h):
    n = h.shape[0]
    st = h.reshape(n, N_PARTS, 2, PART_STATE)
    re = st[:, :, 0].reshape(n, N_SSM_GROUPS, SSM_STATE)
    im = st[:, :, 1].reshape(n, N_SSM_GROUPS, SSM_STATE)
    return re, im


def _prompt_path(x_prompt, meta_tokens, w):
    n_seq, seq_len, _ = x_prompt.shape
    assert n_seq == P_SEQ and seq_len % P_CHUNK == 0
    n_chunks = seq_len // P_CHUNK + 1
    ins = [x_prompt, meta_tokens, w["g_mix"], w["w_in"], w["sink_prompt"], w["lam_r"], w["lam_i"],
           w["bcat"], w["ccat"], w["dskip"], w["w_glu"], w["b_glu"], w["g_attn"], w["g_ssm"],
           w["wo_a"], w["wo_s"]]
    in_specs = [pl.BlockSpec((P_SEQ, P_CHUNK, D_MODEL), lambda i: (0, jnp.maximum(i - 1, 0), 0))]
    in_specs += [_const(a.shape) for a in ins[1:]]
    x1_tm, k_last, v_last, h_last = pl.pallas_call(
        _prompt_mixer_kernel,
        grid=(n_chunks,),
        in_specs=in_specs,
        out_specs=[pl.BlockSpec((P_ROWS, D_MODEL), lambda i: (i, 0)),
                   _full((P_SEQ, P_CHUNK, KV_WIDTH)), _full((P_SEQ, P_CHUNK, KV_WIDTH)),
                   _full((P_SEQ, STATE_COLS))],
        out_shape=[jax.ShapeDtypeStruct((n_chunks * P_ROWS, D_MODEL), F32),
                   jax.ShapeDtypeStruct((P_SEQ, P_CHUNK, KV_WIDTH), F32),
                   jax.ShapeDtypeStruct((P_SEQ, P_CHUNK, KV_WIDTH), F32),
                   jax.ShapeDtypeStruct((P_SEQ, STATE_COLS), F32)],
        scratch_shapes=[
            pltpu.VMEM((P_ROWS, QKV_COLS), BF16),
            pltpu.VMEM((P_ROWS, 2 * KV_WIDTH), BF16),
            pltpu.VMEM((N_PARTS, P_ROWS, LANES), F32),
            pltpu.VMEM((SCAN_SUB * P_SEQ, STATE_COLS), F32),
            pltpu.VMEM((P_ROWS, ATTN_WIDTH), F32),
            pltpu.VMEM((D_MODEL // LANES, P_ROWS, LANES), F32),
        ],
        compiler_params=_params(("arbitrary",)),
        name="prompt_mixer",
    )(*ins)
    conv0 = jnp.zeros((2 * P_SEQ, D_FF), F32)
    y_prompt, conv_tm = _ffn_call(x1_tm, conv0, w, rows_per_step=P_SEQ, block_rows=P_ROWS,
                                  to_batch_major=True)
    p_re, p_im = _cols_to_state(h_last)
    p_conv = conv_tm.reshape(2, P_SEQ, D_FF).transpose(1, 0, 2)
    kv_shape = (1, P_SEQ, WINDOW, N_KV_HEADS, HEAD_DIM)
    return (y_prompt, k_last.reshape(kv_shape), v_last.reshape(kv_shape), p_re[None], p_im[None],
            p_conv[None])


def _sample_path(x_sample, cache_k, cache_v, st_re, st_im, st_conv, w):
    n_seq, n_steps, _ = x_sample.shape
    n_cache = cache_k.shape[1]
    n_rows = n_steps * n_seq
    seq_batch = 16
    n_state = N_SSM_GROUPS * SSM_STATE
    x_tm = x_sample.transpose(1, 0, 2).reshape(n_rows, D_MODEL)
    h0_re = st_re.transpose(1, 2, 0).reshape(n_state, n_seq)
    h0_im = st_im.transpose(1, 2, 0).reshape(n_state, n_seq)
    ins = [x_tm, h0_re, h0_im, w["g_mix"], w["w_in"], w["lam_r"], w["lam_i"], w["bcat"], w["ccat"],
           w["dskip"], w["w_glu"], w["b_glu"], w["g_ssm"]]
    q, k_new, v_new, ssm_n, h_re, h_im = pl.pallas_call(
        functools.partial(_sample_inproj_kernel, n_steps=n_steps, n_seq=n_seq),
        grid=(1,),
        in_specs=[_full(a.shape) for a in ins],
        out_specs=[_full((n_rows, ATTN_WIDTH)), _full((n_rows, KV_WIDTH)), _full((n_rows, KV_WIDTH)),
                   _full((n_rows, SSM_WIDTH)), _full((n_state, n_seq)), _full((n_state, n_seq))],
        out_shape=[jax.ShapeDtypeStruct((n_rows, ATTN_WIDTH), BF16),
                   jax.ShapeDtypeStruct((n_rows, KV_WIDTH), F32),
                   jax.ShapeDtypeStruct((n_rows, KV_WIDTH), F32),
                   jax.ShapeDtypeStruct((n_rows, SSM_WIDTH), BF16),
                   jax.ShapeDtypeStruct((n_state, n_seq), F32),
                   jax.ShapeDtypeStruct((n_state, n_seq), F32)],
        scratch_shapes=[pltpu.VMEM((N_PARTS, n_rows, LANES), F32),
                        pltpu.VMEM((n_rows, STATE_COLS), F32),
                        pltpu.VMEM((n_seq, STATE_COLS), F32)],
        compiler_params=_params(("arbitrary",)),
        name="sample_inproj",
    )(*ins)

    group = N_HEADS // N_KV_HEADS
    rows = n_steps * seq_batch
    sink_sample = jnp.repeat(w["sinks"].reshape(N_KV_HEADS, group), rows, axis=1)[..., None]
    kc = cache_k.transpose(0, 2, 3, 1)
    vc = cache_v.transpose(0, 2, 3, 1)
    step_block = lambda width: pl.BlockSpec((n_steps, seq_batch, width), lambda b: (0, b, 0))
    cache_block = pl.BlockSpec((seq_batch, N_KV_HEADS, HEAD_DIM, n_cache), lambda b: (b, 0, 0, 0))
    ins = [q.reshape(n_steps, n_seq, ATTN_WIDTH), k_new.reshape(n_steps, n_seq, KV_WIDTH),
           v_new.reshape(n_steps, n_seq, KV_WIDTH), kc, vc, sink_sample, x_tm, ssm_n,
           w["g_attn"], w["wo_a"], w["wo_s"]]
    in_specs = [step_block(ATTN_WIDTH), step_block(KV_WIDTH), step_block(KV_WIDTH), cache_block,
                cache_block] + [_const(a.shape) for a in ins[5:]]
    x1_tm = pl.pallas_call(
        functools.partial(_sample_attn_kernel, n_steps=n_steps, n_seq=n_seq, seq_batch=seq_batch,
                          n_cache=n_cache),
        grid=(n_seq // seq_batch,),
        in_specs=in_specs,
        out_specs=_full((n_rows, D_MODEL)),
        out_shape=jax.ShapeDtypeStruct((n_rows, D_MODEL), F32),
        scratch_shapes=[pltpu.VMEM((n_steps, n_seq, ATTN_WIDTH), F32),
                        pltpu.VMEM((N_QSLABS * rows, seq_batch * n_cache), F32),
                        pltpu.VMEM((N_QSLABS * rows, rows), F32)],
        compiler_params=_params(("arbitrary",)),
        name="sample_attn",
    )(*ins)

    conv0 = st_conv.transpose(1, 0, 2).reshape(2 * n_seq, D_FF)
    y_tm, conv_tm = _ffn_call(x1_tm, conv0, w, rows_per_step=n_seq, block_rows=n_rows,
                              to_batch_major=False)
    y_sample = y_tm.reshape(n_steps, n_seq, D_MODEL).transpose(1, 0, 2)
    s_conv = conv_tm.reshape(2, n_seq, D_FF).transpose(1, 0, 2)
    s_re = h_re.reshape(N_SSM_GROUPS, SSM_STATE, n_seq).transpose(2, 0, 1)
    s_im = h_im.reshape(N_SSM_GROUPS, SSM_STATE, n_seq).transpose(2, 0, 1)

    def window(cache_t, new):
        new_t = new.reshape(n_steps, n_seq, N_KV_HEADS, HEAD_DIM).transpose(1, 2, 3, 0)
        return jnp.concatenate([cache_t[..., n_steps:], new_t], axis=-1).transpose(0, 3, 1, 2)

    return (y_sample, window(kc, k_new)[None], window(vc, v_new)[None], s_re[None], s_im[None],
            s_conv[None])


def kernel(x_prompt, x_sample, cache_k_win, cache_v_win, state_ssm_re, state_ssm_im, state_conv, meta_tokens, g_mix, w_in, sinks, lam_re, lam_im, log_dt, b_re, b_im, c_re, c_im, d_skip, w_glu, b_glu, g_attn_out, g_ssm_out, w_o, g_ffn, w_gate, w_up, conv_w, conv_b, w_down, g_final):
    assert g_mix.shape[0] == 1, "single-layer kernel"
    w = _prep_weights(g_mix[0], w_in[0], sinks[0], lam_re[0], lam_im[0], log_dt[0], b_re[0], b_im[0],
                      c_re[0], c_im[0], d_skip[0], w_glu[0], b_glu[0], g_attn_out[0], g_ssm_out[0],
                      w_o[0], g_ffn[0], w_gate[0], w_up[0], conv_w[0], conv_b[0], w_down[0], g_final)
    y_p, p_k, p_v, p_re, p_im, p_conv = _prompt_path(x_prompt, meta_tokens, w)
    y_s, s_k, s_v, s_re, s_im, s_conv = _sample_path(
        x_sample, cache_k_win[0], cache_v_win[0], state_ssm_re[0], state_ssm_im[0], state_conv[0], w)
    return (y_p, y_s, p_k, p_v, p_re, p_im, p_conv, s_k, s_v, s_re, s_im, s_conv)
```

```python
import functools
import math

import jax
import jax.numpy as jnp
from jax import lax
from jax.experimental import pallas as pl
from jax.experimental.pallas import tpu as pltpu

F32 = jnp.float32
BF16 = jnp.bfloat16

D_MODEL = 1024
N_META = 16
HEAD_DIM = 64
ATTN_WIDTH = 512
N_HEADS = 8
N_KV_HEADS = 2
KV_WIDTH = 128
WINDOW = 128
SSM_WIDTH = 512
SSM_GROUP = 16
N_SSM_GROUPS = 32
SSM_STATE = 64
D_FF = 2816
EPS = 1e-5

LANES = 128
SUBLANES = 8
N_QSLABS = ATTN_WIDTH // LANES
N_PARTS = SSM_WIDTH // LANES
PART_STATE = (LANES // SSM_GROUP) * SSM_STATE
STATE_COLS = N_PARTS * 2 * PART_STATE
FFN_SLICE = 256
N_FFN_SLICES = D_FF // FFN_SLICE
QKV_COLS = ATTN_WIDTH + 2 * KV_WIDTH
VMEM_LIMIT_BYTES = 58 * 1024 * 1024

P_CHUNK = 128
P_SEQ = 8
P_ROWS = P_CHUNK * P_SEQ
P_FRONT = P_CHUNK - N_META
SCAN_SUB = 32


def _rms(x, g):
    return x * lax.rsqrt(jnp.mean(x * x, axis=-1, keepdims=True) + EPS) * g


def _dot(a, b):
    return jnp.dot(a, b, preferred_element_type=F32)


def _dot_t(a, b):
    return lax.dot_general(a, b, (((1,), (1,)), ((), ())), preferred_element_type=F32)


def _lane_is_left(shape):
    return lax.broadcasted_iota(jnp.int32, shape, len(shape) - 1) < HEAD_DIM


def _s5_scan(bu_ref, h_ref, lam_r_ref, lam_i_ref, n_steps, rows_per_step, unroll):
    n_tiles = rows_per_step // SUBLANES

    def tile_body(r, carry):
        r0 = pl.multiple_of(r * SUBLANES, SUBLANES)
        for k in range(N_PARTS):
            c_re = k * 2 * PART_STATE
            c_im = c_re + PART_STATE
            lr = jnp.broadcast_to(lam_r_ref[:, k * PART_STATE:(k + 1) * PART_STATE],
                                  (SUBLANES, PART_STATE))
            li = jnp.broadcast_to(lam_i_ref[:, k * PART_STATE:(k + 1) * PART_STATE],
                                  (SUBLANES, PART_STATE))
            hr = h_ref[pl.ds(r0, SUBLANES), c_re:c_re + PART_STATE]
            hi = h_ref[pl.ds(r0, SUBLANES), c_im:c_im + PART_STATE]

            def step(t, h):
                hr, hi = h
                row = pl.multiple_of(t * rows_per_step + r0, SUBLANES)
                br = bu_ref[pl.ds(row, SUBLANES), c_re:c_re + PART_STATE]
                bi = bu_ref[pl.ds(row, SUBLANES), c_im:c_im + PART_STATE]
                nr = lr * hr - li * hi + br
                ni = lr * hi + li * hr + bi
                bu_ref[pl.ds(row, SUBLANES), c_re:c_re + PART_STATE] = nr
                bu_ref[pl.ds(row, SUBLANES), c_im:c_im + PART_STATE] = ni
                return nr, ni

            hr, hi = lax.fori_loop(0, n_steps, step, (hr, hi), unroll=unroll)
            h_ref[pl.ds(r0, SUBLANES), c_re:c_re + PART_STATE] = hr
            h_ref[pl.ds(r0, SUBLANES), c_im:c_im + PART_STATE] = hi
        return carry

    if n_tiles == 1:
        tile_body(0, 0)
    else:
        lax.fori_loop(0, n_tiles, tile_body, 0)


def _s5_block(u_tm_ref, row0, n_rows, bu_ref, h_ref, lam_r_ref, lam_i_ref,
              bcat_ref, ccat_ref, dskip_ref, n_steps, rows_per_step, unroll):
    for k in range(N_PARTS):
        u_k = u_tm_ref[k, pl.ds(row0, n_rows), :].astype(BF16)
        bu_ref[:, k * 2 * PART_STATE:(k + 1) * 2 * PART_STATE] = _dot(u_k, bcat_ref[k])
    _s5_scan(bu_ref, h_ref, lam_r_ref, lam_i_ref, n_steps, rows_per_step, unroll)
    for k in range(N_PARTS):
        h_k = bu_ref[:, k * 2 * PART_STATE:(k + 1) * 2 * PART_STATE].astype(BF16)
        y_k = _dot(h_k, ccat_ref[k])
        u_k = u_tm_ref[k, pl.ds(row0, n_rows), :]
        u_tm_ref[k, pl.ds(row0, n_rows), :] = y_k + dskip_ref[:, k * LANES:(k + 1) * LANES] * u_k


def _glu_norm(y, w_glu_ref, b_glu_ref, g_ssm_ref):
    g = jax.nn.gelu(y)
    gate = jax.nn.sigmoid(_dot(g.astype(BF16), w_glu_ref[...]) + b_glu_ref[...])
    return _rms(g * gate, g_ssm_ref[...])


def _softmax_sink(parts, sink):
    m = sink
    for s in parts:
        m = jnp.maximum(m, jnp.max(s, axis=-1, keepdims=True))
    ps = [jnp.exp(s - m) for s in parts]
    denom = jnp.exp(sink - m)
    for p in ps:
        denom = denom + jnp.sum(p, axis=-1, keepdims=True)
    return ps, denom


def _prompt_mixer_kernel(x_ref, meta_ref, g_mix_ref, w_in_ref, sink_ref, lam_r_ref, lam_i_ref,
                         bcat_ref, ccat_ref, dskip_ref, w_glu_ref, b_glu_ref, g_attn_ref,
                         g_ssm_ref, wo_a_ref, wo_s_ref,
                         x1_ref, k_out_ref, v_out_ref, h_ref,
                         qkv_sc, kvprev_sc, u_tm, bu_sc, o_sc, xt_sc):
    i = pl.program_id(0)

    @pl.when(i == 0)
    def _():
        h_ref[...] = jnp.zeros_like(h_ref)
        kvprev_sc[...] = jnp.zeros_like(kvprev_sc)

    def load_x():
        first = jnp.concatenate([jnp.zeros((P_FRONT, D_MODEL), F32), meta_ref[...]], axis=0)
        first = jnp.broadcast_to(first[None], (P_SEQ, P_CHUNK, D_MODEL))
        return jnp.where(i == 0, first, x_ref[...]).reshape(P_ROWS, D_MODEL)

    xn = _rms(load_x(), g_mix_ref[...]).astype(BF16)
    qkv = _dot(xn, w_in_ref[:, :QKV_COLS])
    k_out_ref[...] = qkv[:, ATTN_WIDTH:ATTN_WIDTH + KV_WIDTH].reshape(P_SEQ, P_CHUNK, KV_WIDTH)
    v_out_ref[...] = qkv[:, ATTN_WIDTH + KV_WIDTH:].reshape(P_SEQ, P_CHUNK, KV_WIDTH)
    qkv_sc[...] = qkv.astype(BF16)
    u = _dot(xn, w_in_ref[:, QKV_COLS:])
    for s in range(P_SEQ):
        for k in range(N_PARTS):
            u_tm[k, pl.ds(s, P_CHUNK, stride=P_SEQ), :] = (
                u[s * P_CHUNK:(s + 1) * P_CHUNK, k * LANES:(k + 1) * LANES])

    tq = lax.broadcasted_iota(jnp.int32, (2 * P_CHUNK, 2 * P_CHUNK), 0) % P_CHUNK
    col = lax.broadcasted_iota(jnp.int32, (2 * P_CHUNK, 2 * P_CHUNK), 1)
    is_prev = col < P_CHUNK
    tk = jnp.where(is_prev, col, col - P_CHUNK)
    kpos = jnp.where(is_prev, (i - 1) * P_CHUNK, i * P_CHUNK) + tk - P_FRONT
    valid = ((is_prev & (tk > tq)) | (jnp.logical_not(is_prev) & (tk <= tq))) & (kpos >= 0)
    bias = jnp.where(valid, 0.0, -jnp.inf).astype(F32)[:P_CHUNK]
    left = _lane_is_left((P_CHUNK, LANES))
    n_stack = 2 * N_QSLABS

    def attn_seq(s, carry):
        r0 = pl.multiple_of(s * P_CHUNK, P_CHUNK)
        kv_cur = qkv_sc[pl.ds(r0, P_CHUNK), ATTN_WIDTH:]
        kv_cat = jnp.concatenate([kvprev_sc[pl.ds(r0, P_CHUNK), :], kv_cur], axis=0)
        k_cat = kv_cat[:, :KV_WIDTH]
        v_ext = jnp.concatenate([kv_cat[:, KV_WIDTH:], jnp.ones((2 * P_CHUNK, LANES), BF16)], axis=1)
        qs = []
        for j in range(N_QSLABS):
            q = qkv_sc[pl.ds(r0, P_CHUNK), j * LANES:(j + 1) * LANES]
            qs += [jnp.where(left, q, 0), jnp.where(left, 0, q)]
        q_all = jnp.concatenate(qs, axis=0)
        sc = _dot_t(q_all, k_cat).reshape(n_stack, P_CHUNK, 2 * P_CHUNK) + bias[None]
        sc = sc.reshape(n_stack * P_CHUNK, 2 * P_CHUNK)
        sink = sink_ref[...]
        m = jnp.max(jnp.maximum(sc[:, :LANES], sc[:, LANES:]), axis=-1, keepdims=True)
        m = jnp.maximum(jnp.broadcast_to(m, sink.shape), sink)
        p = jnp.concatenate([jnp.exp(sc[:, :LANES] - m), jnp.exp(sc[:, LANES:] - m)], axis=1)
        o_ext = _dot(p.astype(BF16), v_ext)
        o_all = o_ext[:, :LANES] / (o_ext[:, LANES:] + jnp.exp(sink - m))
        for j in range(N_QSLABS):
            o_sc[pl.ds(r0, P_CHUNK), j * LANES:(j + 1) * LANES] = jnp.where(
                left, o_all[2 * j * P_CHUNK:(2 * j + 1) * P_CHUNK],
                o_all[(2 * j + 1) * P_CHUNK:(2 * j + 2) * P_CHUNK])
        return carry

    lax.fori_loop(0, P_SEQ, attn_seq, 0, unroll=2)
    kvprev_sc[...] = qkv_sc[:, ATTN_WIDTH:]

    sub_rows = SCAN_SUB * P_SEQ

    for c in range(P_CHUNK // SCAN_SUB):
        _s5_block(u_tm, c * sub_rows, sub_rows, bu_sc.at[c % 2], h_ref, lam_r_ref, lam_i_ref,
                  bcat_ref, ccat_ref, dskip_ref, SCAN_SUB, P_SEQ, True)
    y_ssm = jnp.concatenate([u_tm[k] for k in range(N_PARTS)], axis=1)
    ssm_n = _glu_norm(y_ssm, w_glu_ref, b_glu_ref, g_ssm_ref)
    mix_tm = _dot(ssm_n.astype(BF16), wo_s_ref[...])

    attn_n = _rms(o_sc[...], g_attn_ref[...])
    x1_bm = load_x() + _dot(attn_n.astype(BF16), wo_a_ref[...])
    for s in range(P_SEQ):
        for j in range(D_MODEL // LANES):
            xt_sc[j, pl.ds(s, P_CHUNK, stride=P_SEQ), :] = (
                x1_bm[s * P_CHUNK:(s + 1) * P_CHUNK, j * LANES:(j + 1) * LANES])
    x1_ref[...] = jnp.concatenate([xt_sc[j] for j in range(D_MODEL // LANES)], axis=1) + mix_tm


def _ffn_kernel(x1_ref, conv0_ref, g_ffn_ref, w_gate_ref, w_up_ref, conv_w_ref, conv_b_ref,
                w_down_ref, g_final_ref, y_ref, conv_ref, h_sc, *rest, rows_per_step, to_batch_major):
    i = pl.program_id(0)
    n_rows = x1_ref.shape[0]
    halo = 2 * rows_per_step

    @pl.when(i == 0)
    def _():
        conv_ref[...] = conv0_ref[...]

    x1 = x1_ref[...]
    xn = _rms(x1, g_ffn_ref[...]).astype(BF16)
    for sl in range(N_FFN_SLICES):
        c0, c1 = sl * FFN_SLICE, (sl + 1) * FFN_SLICE
        g = _dot(xn, w_gate_ref[:, c0:c1])
        up = _dot(xn, w_up_ref[:, c0:c1])
        gp = jnp.concatenate([conv_ref[:, c0:c1], g], axis=0)
        conv = (conv_b_ref[:, c0:c1]
                + conv_w_ref[0:1, c0:c1] * gp[0:n_rows]
                + conv_w_ref[1:2, c0:c1] * gp[rows_per_step:rows_per_step + n_rows]
                + conv_w_ref[2:3, c0:c1] * gp[halo:halo + n_rows])
        conv_ref[:, c0:c1] = gp[n_rows:n_rows + halo]
        h_sc[:, c0:c1] = (jax.nn.silu(conv) * up).astype(BF16)
    x2 = x1 + _dot(h_sc[...], w_down_ref[...])
    y = _rms(x2, g_final_ref[...])
    if to_batch_major:
        (yt_sc,) = rest
        n_seq = rows_per_step
        n_t = n_rows // n_seq
        for j in range(D_MODEL // LANES):
            yt_sc[j] = y[:, j * LANES:(j + 1) * LANES]
        for s in range(n_seq):
            for j in range(D_MODEL // LANES):
                y_ref[s, :, j * LANES:(j + 1) * LANES] = yt_sc[j, pl.ds(s, n_t, stride=n_seq), :]
    else:
        y_ref[...] = y


def _sample_inproj_kernel(x_ref, h0_re_ref, h0_im_ref, g_mix_ref, w_in_ref, lam_r_ref, lam_i_ref,
                          bcat_ref, ccat_ref, dskip_ref, w_glu_ref, b_glu_ref, g_ssm_ref,
                          q_ref, k_ref, v_ref, ssm_ref, h_re_ref, h_im_ref, u_tm, bu_sc, h_sc,
                          *, n_steps, n_seq):
    n_rows = n_steps * n_seq
    xn = _rms(x_ref[...], g_mix_ref[...]).astype(BF16)
    qkv = _dot(xn, w_in_ref[:, :QKV_COLS])
    q_ref[...] = qkv[:, :ATTN_WIDTH].astype(BF16)
    k_ref[...] = qkv[:, ATTN_WIDTH:ATTN_WIDTH + KV_WIDTH]
    v_ref[...] = qkv[:, ATTN_WIDTH + KV_WIDTH:]
    u = _dot(xn, w_in_ref[:, QKV_COLS:])
    for k in range(N_PARTS):
        u_tm[k] = u[:, k * LANES:(k + 1) * LANES]
    h0_re = h0_re_ref[...].T
    h0_im = h0_im_ref[...].T
    for k in range(N_PARTS):
        h_sc[:, k * 2 * PART_STATE:k * 2 * PART_STATE + PART_STATE] = (
            h0_re[:, k * PART_STATE:(k + 1) * PART_STATE])
        h_sc[:, k * 2 * PART_STATE + PART_STATE:(k + 1) * 2 * PART_STATE] = (
            h0_im[:, k * PART_STATE:(k + 1) * PART_STATE])
    _s5_block(u_tm, 0, n_rows, bu_sc, h_sc, lam_r_ref, lam_i_ref, bcat_ref, ccat_ref,
              dskip_ref, n_steps, n_seq, True)
    h_re_ref[...] = jnp.concatenate(
        [h_sc[:, k * 2 * PART_STATE:k * 2 * PART_STATE + PART_STATE] for k in range(N_PARTS)], axis=1).T
    h_im_ref[...] = jnp.concatenate(
        [h_sc[:, k * 2 * PART_STATE + PART_STATE:(k + 1) * 2 * PART_STATE] for k in range(N_PARTS)],
        axis=1).T
    y_ssm = jnp.concatenate([u_tm[k] for k in range(N_PARTS)], axis=1)
    ssm_ref[...] = _glu_norm(y_ssm, w_glu_ref, b_glu_ref, g_ssm_ref).astype(BF16)


def _sample_attn_kernel(q_ref, kn_ref, vn_ref, kc_ref, vc_ref, sink_ref, x_ref, ssm_ref,
                        g_attn_ref, wo_a_ref, wo_s_ref, x1_ref, o_sc, bias_c_sc, bias_n_sc,
                        *, n_steps, n_seq, seq_batch, n_cache):
    b = pl.program_id(0)
    rows = n_steps * seq_batch
    q_rows = N_QSLABS * rows

    @pl.when(b == 0)
    def _():
        r = lax.broadcasted_iota(jnp.int32, (q_rows, seq_batch * n_cache), 0)
        c = lax.broadcasted_iota(jnp.int32, (q_rows, seq_batch * n_cache), 1)
        t = (r % rows) // seq_batch
        ok = ((r % seq_batch) == (c // n_cache)) & ((c % n_cache) > t)
        bias_c_sc[...] = jnp.where(ok, 0.0, -jnp.inf).astype(F32)
        r = lax.broadcasted_iota(jnp.int32, (q_rows, rows), 0)
        c = lax.broadcasted_iota(jnp.int32, (q_rows, rows), 1)
        t = (r % rows) // seq_batch
        ok = ((r % seq_batch) == (c % seq_batch)) & ((c // seq_batch) <= t)
        bias_n_sc[...] = jnp.where(ok, 0.0, -jnp.inf).astype(F32)

    q = q_ref[...].reshape(rows, ATTN_WIDTH)
    kc = kc_ref[...].astype(BF16).reshape(seq_batch, KV_WIDTH, n_cache)
    vc = vc_ref[...].astype(BF16).reshape(seq_batch, KV_WIDTH, n_cache)
    kc = jnp.concatenate([kc[s] for s in range(seq_batch)], axis=1)
    vc = jnp.concatenate([vc[s] for s in range(seq_batch)], axis=1)
    kn = kn_ref[...].reshape(rows, KV_WIDTH).astype(BF16)
    vn = vn_ref[...].reshape(rows, KV_WIDTH).astype(BF16)
    left = _lane_is_left((rows, LANES))
    halves = []
    for grp in range(N_KV_HEADS):
        qs = []
        for j in range(N_QSLABS):
            qj = q[:, j * LANES:(j + 1) * LANES]
            qs.append(jnp.where(left, qj, 0) if grp == 0 else jnp.where(left, 0, qj))
        qg = jnp.concatenate(qs, axis=0)
        sc_c = _dot(qg, kc) + bias_c_sc[...]
        sc_n = _dot_t(qg, kn) + bias_n_sc[...]
        (p_c, p_n), denom = _softmax_sink([sc_c, sc_n], sink_ref[grp])
        halves.append((_dot_t(p_c.astype(BF16), vc) + _dot(p_n.astype(BF16), vn)) / denom)
    for j in range(N_QSLABS):
        o_j = jnp.where(left, halves[0][j * rows:(j + 1) * rows], halves[1][j * rows:(j + 1) * rows])
        s0 = pl.multiple_of(b * seq_batch, seq_batch)
        o_sc[:, pl.ds(s0, seq_batch), j * LANES:(j + 1) * LANES] = (
            o_j.reshape(n_steps, seq_batch, LANES))

    @pl.when(b == pl.num_programs(0) - 1)
    def _():
        attn_n = _rms(o_sc[...].reshape(n_steps * n_seq, ATTN_WIDTH), g_attn_ref[...])
        x1_ref[...] = (x_ref[...] + _dot(attn_n.astype(BF16), wo_a_ref[...])
                       + _dot(ssm_ref[...], wo_s_ref[...]))


def _full(shape):
    return pl.BlockSpec(shape, lambda *_: (0,) * len(shape))


def _const(shape):
    return pl.BlockSpec(shape, lambda *_: (0,) * len(shape), pipeline_mode=pl.Buffered(1))


def _params(semantics):
    return pltpu.CompilerParams(dimension_semantics=semantics, vmem_limit_bytes=VMEM_LIMIT_BYTES)


def _ffn_call(x1_tm, conv0, w, *, rows_per_step, block_rows, to_batch_major):
    n_rows = x1_tm.shape[0]
    n_blocks = n_rows // block_rows
    halo = 2 * rows_per_step
    in_specs = [
        pl.BlockSpec((block_rows, D_MODEL), lambda i: (i, 0)),
        _const((halo, D_FF)), _const((1, D_MODEL)), _const((D_MODEL, D_FF)), _const((D_MODEL, D_FF)),
        _const((3, D_FF)), _const((1, D_FF)), _const((D_FF, D_MODEL)), _const((1, D_MODEL)),
    ]
    scratch = [pltpu.VMEM((block_rows, D_FF), BF16)]
    if to_batch_major:
        n_t = block_rows // rows_per_step
        y_shape = jax.ShapeDtypeStruct((rows_per_step, (n_blocks - 1) * n_t, D_MODEL), F32)
        y_spec = pl.BlockSpec((rows_per_step, n_t, D_MODEL), lambda i: (0, jnp.maximum(i - 1, 0), 0))
        scratch.append(pltpu.VMEM((D_MODEL // LANES, block_rows, LANES), F32))
    else:
        y_shape = jax.ShapeDtypeStruct((n_rows, D_MODEL), F32)
        y_spec = pl.BlockSpec((block_rows, D_MODEL), lambda i: (i, 0))
    return pl.pallas_call(
        functools.partial(_ffn_kernel, rows_per_step=rows_per_step, to_batch_major=to_batch_major),
        grid=(n_blocks,),
        in_specs=in_specs,
        out_specs=[y_spec, _full((halo, D_FF))],
        out_shape=[y_shape, jax.ShapeDtypeStruct((halo, D_FF), F32)],
        scratch_shapes=scratch,
        compiler_params=_params(("arbitrary",)),
        name="conv_ffn",
    )(x1_tm, conv0, w["g_ffn"], w["w_gate"], w["w_up"], w["conv_w"], w["conv_b"], w["w_down"],
      w["g_final"])


def _pair_heads(x, axis):
    group = N_HEADS // N_KV_HEADS
    shape = x.shape
    x = x.reshape(shape[:axis] + (N_KV_HEADS, group, HEAD_DIM) + shape[axis + 1:])
    x = jnp.swapaxes(x, axis, axis + 1)
    return x.reshape(shape)


def _prep_weights(g_mix, w_in, sinks, lam_re, lam_im, log_dt, b_re, b_im, c_re, c_im, d_skip,
                  w_glu, b_glu, g_attn_out, g_ssm_out, w_o, g_ffn, w_gate, w_up, conv_w, conv_b,
                  w_down, g_final):
    group = N_HEADS // N_KV_HEADS
    w_q = _pair_heads(w_in[:, :ATTN_WIDTH], 1) * (HEAD_DIM ** -0.5)
    w = {
        "g_mix": g_mix.reshape(1, D_MODEL),
        "w_in": jnp.concatenate([w_q, w_in[:, ATTN_WIDTH:]], axis=1).astype(BF16),
        "g_attn": _pair_heads(g_attn_out, 0).reshape(1, ATTN_WIDTH),
        "g_ssm": g_ssm_out.reshape(1, SSM_WIDTH),
        "wo_a": _pair_heads(w_o[:ATTN_WIDTH], 0).astype(BF16),
        "wo_s": w_o[ATTN_WIDTH:].astype(BF16),
        "w_glu": w_glu.astype(BF16),
        "b_glu": b_glu.reshape(1, SSM_WIDTH),
        "dskip": d_skip.reshape(1, SSM_WIDTH),
        "g_ffn": g_ffn.reshape(1, D_MODEL),
        "w_gate": w_gate.astype(BF16),
        "w_up": w_up.astype(BF16),
        "conv_w": conv_w,
        "conv_b": conv_b.reshape(1, D_FF),
        "w_down": w_down.astype(BF16),
        "g_final": g_final.reshape(1, D_MODEL),
    }
    dt = jnp.exp(log_dt)[:, None]
    mag = jnp.exp(lam_re * dt)
    lb_r = mag * jnp.cos(lam_im * dt)
    lb_i = mag * jnp.sin(lam_im * dt)
    inv = 1.0 / (lam_re * lam_re + lam_im * lam_im)
    cf_r = ((lb_r - 1.0) * lam_re + lb_i * lam_im) * inv
    cf_i = (lb_i * lam_re - (lb_r - 1.0) * lam_im) * inv
    bb_r = cf_r[..., None] * b_re - cf_i[..., None] * b_im
    bb_i = cf_r[..., None] * b_im + cf_i[..., None] * b_re
    w["lam_r"] = lb_r.reshape(1, N_SSM_GROUPS * SSM_STATE)
    w["lam_i"] = lb_i.reshape(1, N_SSM_GROUPS * SSM_STATE)
    gpp = LANES // SSM_GROUP
    eye = jnp.eye(gpp, dtype=F32)

    def b_part(b):
        b = b.reshape(N_PARTS, gpp, SSM_STATE, SSM_GROUP).transpose(0, 1, 3, 2)
        b = b[:, :, :, None, :] * eye[None, :, None, :, None]
        return b.reshape(N_PARTS, LANES, PART_STATE)

    def c_part(c):
        c = c.reshape(N_PARTS, gpp, SSM_GROUP, SSM_STATE).transpose(0, 1, 3, 2)
        c = c[:, :, :, None, :] * eye[None, :, None, :, None]
        return c.reshape(N_PARTS, PART_STATE, LANES)

    w["bcat"] = jnp.concatenate([b_part(bb_r), b_part(bb_i)], axis=2).astype(BF16)
    w["ccat"] = jnp.concatenate([c_part(c_re), -c_part(c_im)], axis=1).astype(BF16)
    sink_rows = jnp.repeat(sinks.reshape(N_KV_HEADS, group).T.reshape(N_HEADS), P_CHUNK)
    w["sink_prompt"] = jnp.broadcast_to(sink_rows[:, None], (N_HEADS * P_CHUNK, LANES))
    w["sinks"] = sinks
    return w


def _cols_to_state(h):
    n = h.shape[0]
    st = h.reshape(n, N_PARTS, 2, PART_STATE)
    re = st[:, :, 0].reshape(n, N_SSM_GROUPS, SSM_STATE)
    im = st[:, :, 1].reshape(n, N_SSM_GROUPS, SSM_STATE)
    return re, im


def _prompt_path(x_prompt, meta_tokens, w):
    n_seq, seq_len, _ = x_prompt.shape
    assert n_seq == P_SEQ and seq_len % P_CHUNK == 0
    n_chunks = seq_len // P_CHUNK + 1
    ins = [x_prompt, meta_tokens, w["g_mix"], w["w_in"], w["sink_prompt"], w["lam_r"], w["lam_i"],
           w["bcat"], w["ccat"], w["dskip"], w["w_glu"], w["b_glu"], w["g_attn"], w["g_ssm"],
           w["wo_a"], w["wo_s"]]
    in_specs = [pl.BlockSpec((P_SEQ, P_CHUNK, D_MODEL), lambda i: (0, jnp.maximum(i - 1, 0), 0))]
    in_specs += [_const(a.shape) for a in ins[1:]]
    x1_tm, k_last, v_last, h_last = pl.pallas_call(
        _prompt_mixer_kernel,
        grid=(n_chunks,),
        in_specs=in_specs,
        out_specs=[pl.BlockSpec((P_ROWS, D_MODEL), lambda i: (i, 0)),
                   _full((P_SEQ, P_CHUNK, KV_WIDTH)), _full((P_SEQ, P_CHUNK, KV_WIDTH)),
                   _full((P_SEQ, STATE_COLS))],
        out_shape=[jax.ShapeDtypeStruct((n_chunks * P_ROWS, D_MODEL), F32),
                   jax.ShapeDtypeStruct((P_SEQ, P_CHUNK, KV_WIDTH), F32),
                   jax.ShapeDtypeStruct((P_SEQ, P_CHUNK, KV_WIDTH), F32),
                   jax.ShapeDtypeStruct((P_SEQ, STATE_COLS), F32)],
        scratch_shapes=[
            pltpu.VMEM((P_ROWS, QKV_COLS), BF16),
            pltpu.VMEM((P_ROWS, 2 * KV_WIDTH), BF16),
            pltpu.VMEM((N_PARTS, P_ROWS, LANES), F32),
            pltpu.VMEM((2, SCAN_SUB * P_SEQ, STATE_COLS), F32),
            pltpu.VMEM((P_ROWS, ATTN_WIDTH), F32),
            pltpu.VMEM((D_MODEL // LANES, P_ROWS, LANES), F32),
        ],
        compiler_params=_params(("arbitrary",)),
        name="prompt_mixer",
    )(*ins)
    conv0 = jnp.zeros((2 * P_SEQ, D_FF), F32)
    y_prompt, conv_tm = _ffn_call(x1_tm, conv0, w, rows_per_step=P_SEQ, block_rows=P_ROWS,
                                  to_batch_major=True)
    p_re, p_im = _cols_to_state(h_last)
    p_conv = conv_tm.reshape(2, P_SEQ, D_FF).transpose(1, 0, 2)
    kv_shape = (1, P_SEQ, WINDOW, N_KV_HEADS, HEAD_DIM)
    return (y_prompt, k_last.reshape(kv_shape), v_last.reshape(kv_shape), p_re[None], p_im[None],
            p_conv[None])


def _sample_path(x_sample, cache_k, cache_v, st_re, st_im, st_conv, w):
    n_seq, n_steps, _ = x_sample.shape
    n_cache = cache_k.shape[1]
    n_rows = n_steps * n_seq
    seq_batch = 16
    n_state = N_SSM_GROUPS * SSM_STATE
    x_tm = x_sample.transpose(1, 0, 2).reshape(n_rows, D_MODEL)
    h0_re = st_re.transpose(1, 2, 0).reshape(n_state, n_seq)
    h0_im = st_im.transpose(1, 2, 0).reshape(n_state, n_seq)
    ins = [x_tm, h0_re, h0_im, w["g_mix"], w["w_in"], w["lam_r"], w["lam_i"], w["bcat"], w["ccat"],
           w["dskip"], w["w_glu"], w["b_glu"], w["g_ssm"]]
    q, k_new, v_new, ssm_n, h_re, h_im = pl.pallas_call(
        functools.partial(_sample_inproj_kernel, n_steps=n_steps, n_seq=n_seq),
        grid=(1,),
        in_specs=[_full(a.shape) for a in ins],
        out_specs=[_full((n_rows, ATTN_WIDTH)), _full((n_rows, KV_WIDTH)), _full((n_rows, KV_WIDTH)),
                   _full((n_rows, SSM_WIDTH)), _full((n_state, n_seq)), _full((n_state, n_seq))],
        out_shape=[jax.ShapeDtypeStruct((n_rows, ATTN_WIDTH), BF16),
                   jax.ShapeDtypeStruct((n_rows, KV_WIDTH), F32),
                   jax.ShapeDtypeStruct((n_rows, KV_WIDTH), F32),
                   jax.ShapeDtypeStruct((n_rows, SSM_WIDTH), BF16),
                   jax.ShapeDtypeStruct((n_state, n_seq), F32),
                   jax.ShapeDtypeStruct((n_state, n_seq), F32)],
        scratch_shapes=[pltpu.VMEM((N_PARTS, n_rows, LANES), F32),
                        pltpu.VMEM((n_rows, STATE_COLS), F32),
                        pltpu.VMEM((n_seq, STATE_COLS), F32)],
        compiler_params=_params(("arbitrary",)),
        name="sample_inproj",
    )(*ins)

    group = N_HEADS // N_KV_HEADS
    rows = n_steps * seq_batch
    sink_sample = jnp.repeat(w["sinks"].reshape(N_KV_HEADS, group), rows, axis=1)[..., None]
    kc = cache_k.transpose(0, 2, 3, 1)
    vc = cache_v.transpose(0, 2, 3, 1)
    step_block = lambda width: pl.BlockSpec((n_steps, seq_batch, width), lambda b: (0, b, 0))
    cache_block = pl.BlockSpec((seq_batch, N_KV_HEADS, HEAD_DIM, n_cache), lambda b: (b, 0, 0, 0))
    ins = [q.reshape(n_steps, n_seq, ATTN_WIDTH), k_new.reshape(n_steps, n_seq, KV_WIDTH),
           v_new.reshape(n_steps, n_seq, KV_WIDTH), kc, vc, sink_sample, x_tm, ssm_n,
           w["g_attn"], w["wo_a"], w["wo_s"]]
    in_specs = [step_block(ATTN_WIDTH), step_block(KV_WIDTH), step_block(KV_WIDTH), cache_block,
                cache_block] + [_const(a.shape) for a in ins[5:]]
    x1_tm = pl.pallas_call(
        functools.partial(_sample_attn_kernel, n_steps=n_steps, n_seq=n_seq, seq_batch=seq_batch,
                          n_cache=n_cache),
        grid=(n_seq // seq_batch,),
        in_specs=in_specs,
        out_specs=_full((n_rows, D_MODEL)),
        out_shape=jax.ShapeDtypeStruct((n_rows, D_MODEL), F32),
        scratch_shapes=[pltpu.VMEM((n_steps, n_seq, ATTN_WIDTH), F32),
                        pltpu.VMEM((N_QSLABS * rows, seq_batch * n_cache), F32),
                        pltpu.VMEM((N_QSLABS * rows, rows), F32)],
        compiler_params=_params(("arbitrary",)),
        name="sample_attn",
    )(*ins)

    conv0 = st_conv.transpose(1, 0, 2).reshape(2 * n_seq, D_FF)
    y_tm, conv_tm = _ffn_call(x1_tm, conv0, w, rows_per_step=n_seq, block_rows=n_rows,
                              to_batch_major=False)
    y_sample = y_tm.reshape(n_steps, n_seq, D_MODEL).transpose(1, 0, 2)
    s_conv = conv_tm.reshape(2, n_seq, D_FF).transpose(1, 0, 2)
    s_re = h_re.reshape(N_SSM_GROUPS, SSM_STATE, n_seq).transpose(2, 0, 1)
    s_im = h_im.reshape(N_SSM_GROUPS, SSM_STATE, n_seq).transpose(2, 0, 1)

    def window(cache_t, new):
        new_t = new.reshape(n_steps, n_seq, N_KV_HEADS, HEAD_DIM).transpose(1, 2, 3, 0)
        return jnp.concatenate([cache_t[..., n_steps:], new_t], axis=-1).transpose(0, 3, 1, 2)

    return (y_sample, window(kc, k_new)[None], window(vc, v_new)[None], s_re[None], s_im[None],
            s_conv[None])


def kernel(x_prompt, x_sample, cache_k_win, cache_v_win, state_ssm_re, state_ssm_im, state_conv, meta_tokens, g_mix, w_in, sinks, lam_re, lam_im, log_dt, b_re, b_im, c_re, c_im, d_skip, w_glu, b_glu, g_attn_out, g_ssm_out, w_o, g_ffn, w_gate, w_up, conv_w, conv_b, w_down, g_final):
    assert g_mix.shape[0] == 1, "single-layer kernel"
    w = _prep_weights(g_mix[0], w_in[0], sinks[0], lam_re[0], lam_im[0], log_dt[0], b_re[0], b_im[0],
                      c_re[0], c_im[0], d_skip[0], w_glu[0], b_glu[0], g_attn_out[0], g_ssm_out[0],
                      w_o[0], g_ffn[0], w_gate[0], w_up[0], conv_w[0], conv_b[0], w_down[0], g_final)
    y_p, p_k, p_v, p_re, p_im, p_conv = _prompt_path(x_prompt, meta_tokens, w)
    y_s, s_k, s_v, s_re, s_im, s_conv = _sample_path(
        x_sample, cache_k_win[0], cache_v_win[0], state_ssm_re[0], state_ssm_im[0], state_conv[0], w)
    return (y_p, y_s, p_k, p_v, p_re, p_im, p_conv, s_k, s_v, s_re, s_im, s_conv)
```

```python
import functools
import math

import jax
import jax.numpy as jnp
from jax import lax
from jax.experimental import pallas as pl
from jax.experimental.pallas import tpu as pltpu

F32 = jnp.float32
BF16 = jnp.bfloat16

D_MODEL = 1024
N_META = 16
HEAD_DIM = 64
ATTN_WIDTH = 512
N_HEADS = 8
N_KV_HEADS = 2
KV_WIDTH = 128
WINDOW = 128
SSM_WIDTH = 512
SSM_GROUP = 16
N_SSM_GROUPS = 32
SSM_STATE = 64
D_FF = 2816
EPS = 1e-5

LANES = 128
SUBLANES = 8
N_QSLABS = ATTN_WIDTH // LANES
N_PARTS = SSM_WIDTH // LANES
PART_STATE = (LANES // SSM_GROUP) * SSM_STATE
STATE_COLS = N_PARTS * 2 * PART_STATE
FFN_SLICE = 256
N_FFN_SLICES = D_FF // FFN_SLICE
QKV_COLS = ATTN_WIDTH + 2 * KV_WIDTH
VMEM_LIMIT_BYTES = 58 * 1024 * 1024

P_CHUNK = 128
P_SEQ = 8
P_ROWS = P_CHUNK * P_SEQ
P_FRONT = P_CHUNK - N_META
SCAN_SUB = 32


def _rms(x, g):
    return x * lax.rsqrt(jnp.mean(x * x, axis=-1, keepdims=True) + EPS) * g


def _dot(a, b):
    return jnp.dot(a, b, preferred_element_type=F32)


def _dot_t(a, b):
    return lax.dot_general(a, b, (((1,), (1,)), ((), ())), preferred_element_type=F32)


def _lane_is_left(shape):
    return lax.broadcasted_iota(jnp.int32, shape, len(shape) - 1) < HEAD_DIM


def _s5_scan(bu_ref, h_ref, lam_r_ref, lam_i_ref, n_steps, rows_per_step, unroll):
    n_tiles = rows_per_step // SUBLANES

    def tile_body(r, carry):
        r0 = pl.multiple_of(r * SUBLANES, SUBLANES)
        for k in range(N_PARTS):
            c_re = k * 2 * PART_STATE
            c_im = c_re + PART_STATE
            lr = jnp.broadcast_to(lam_r_ref[:, k * PART_STATE:(k + 1) * PART_STATE],
                                  (SUBLANES, PART_STATE))
            li = jnp.broadcast_to(lam_i_ref[:, k * PART_STATE:(k + 1) * PART_STATE],
                                  (SUBLANES, PART_STATE))
            hr = h_ref[pl.ds(r0, SUBLANES), c_re:c_re + PART_STATE]
            hi = h_ref[pl.ds(r0, SUBLANES), c_im:c_im + PART_STATE]

            def step(t, h):
                hr, hi = h
                row = pl.multiple_of(t * rows_per_step + r0, SUBLANES)
                br = bu_ref[pl.ds(row, SUBLANES), c_re:c_re + PART_STATE]
                bi = bu_ref[pl.ds(row, SUBLANES), c_im:c_im + PART_STATE]
                nr = lr * hr - li * hi + br
                ni = lr * hi + li * hr + bi
                bu_ref[pl.ds(row, SUBLANES), c_re:c_re + PART_STATE] = nr
                bu_ref[pl.ds(row, SUBLANES), c_im:c_im + PART_STATE] = ni
                return nr, ni

            hr, hi = lax.fori_loop(0, n_steps, step, (hr, hi), unroll=unroll)
            h_ref[pl.ds(r0, SUBLANES), c_re:c_re + PART_STATE] = hr
            h_ref[pl.ds(r0, SUBLANES), c_im:c_im + PART_STATE] = hi
        return carry

    if n_tiles == 1:
        tile_body(0, 0)
    else:
        lax.fori_loop(0, n_tiles, tile_body, 0)


def _s5_block(u_tm_ref, row0, n_rows, bu_ref, h_ref, lam_r_ref, lam_i_ref,
              bcat_ref, ccat_ref, dskip_ref, n_steps, rows_per_step, unroll):
    for k in range(N_PARTS):
        u_k = u_tm_ref[k, pl.ds(row0, n_rows), :].astype(BF16)
        bu_ref[:, k * 2 * PART_STATE:(k + 1) * 2 * PART_STATE] = _dot(u_k, bcat_ref[k])
    _s5_scan(bu_ref, h_ref, lam_r_ref, lam_i_ref, n_steps, rows_per_step, unroll)
    for k in range(N_PARTS):
        h_k = bu_ref[:, k * 2 * PART_STATE:(k + 1) * 2 * PART_STATE].astype(BF16)
        y_k = _dot(h_k, ccat_ref[k])
        u_k = u_tm_ref[k, pl.ds(row0, n_rows), :]
        u_tm_ref[k, pl.ds(row0, n_rows), :] = y_k + dskip_ref[:, k * LANES:(k + 1) * LANES] * u_k


def _glu_norm(y, w_glu_ref, b_glu_ref, g_ssm_ref):
    g = jax.nn.gelu(y)
    gate = jax.nn.sigmoid(_dot(g.astype(BF16), w_glu_ref[...]) + b_glu_ref[...])
    return _rms(g * gate, g_ssm_ref[...])


def _softmax_sink(parts, sink):
    m = sink
    for s in parts:
        m = jnp.maximum(m, jnp.max(s, axis=-1, keepdims=True))
    ps = [jnp.exp(s - m) for s in parts]
    denom = jnp.exp(sink - m)
    for p in ps:
        denom = denom + jnp.sum(p, axis=-1, keepdims=True)
    return ps, denom


def _prompt_mixer_kernel(x_ref, meta_ref, g_mix_ref, w_in_ref, sink_ref, lam_r_ref, lam_i_ref,
                         bcat_ref, ccat_ref, dskip_ref, w_glu_ref, b_glu_ref, g_attn_ref,
                         g_ssm_ref, wo_a_ref, wo_s_ref,
                         x1_ref, k_out_ref, v_out_ref, h_ref,
                         qkv_sc, u_tm, bu_sc, o_sc, xt_sc):
    i = pl.program_id(0)
    cur = qkv_sc.at[i % 2]
    prev = qkv_sc.at[(i + 1) % 2]

    @pl.when(i == 0)
    def _():
        h_ref[...] = jnp.zeros_like(h_ref)
        qkv_sc[1] = jnp.zeros(qkv_sc.shape[1:], BF16)

    def load_x():
        first = jnp.concatenate([jnp.zeros((P_FRONT, D_MODEL), F32), meta_ref[...]], axis=0)
        first = jnp.broadcast_to(first[None], (P_SEQ, P_CHUNK, D_MODEL))
        return jnp.where(i == 0, first, x_ref[...]).reshape(P_ROWS, D_MODEL)

    xn = _rms(load_x(), g_mix_ref[...]).astype(BF16)
    qkv = _dot(xn, w_in_ref[:, :QKV_COLS])
    k_out_ref[...] = qkv[:, ATTN_WIDTH:ATTN_WIDTH + KV_WIDTH].reshape(P_SEQ, P_CHUNK, KV_WIDTH)
    v_out_ref[...] = qkv[:, ATTN_WIDTH + KV_WIDTH:].reshape(P_SEQ, P_CHUNK, KV_WIDTH)
    cur[...] = qkv.astype(BF16)
    u = _dot(xn, w_in_ref[:, QKV_COLS:])
    for s in range(P_SEQ):
        for k in range(N_PARTS):
            u_tm[k, pl.ds(s, P_CHUNK, stride=P_SEQ), :] = (
                u[s * P_CHUNK:(s + 1) * P_CHUNK, k * LANES:(k + 1) * LANES])

    tq = lax.broadcasted_iota(jnp.int32, (2 * P_CHUNK, 2 * P_CHUNK), 0) % P_CHUNK
    col = lax.broadcasted_iota(jnp.int32, (2 * P_CHUNK, 2 * P_CHUNK), 1)
    is_prev = col < P_CHUNK
    tk = jnp.where(is_prev, col, col - P_CHUNK)
    kpos = jnp.where(is_prev, (i - 1) * P_CHUNK, i * P_CHUNK) + tk - P_FRONT
    valid = ((is_prev & (tk > tq)) | (jnp.logical_not(is_prev) & (tk <= tq))) & (kpos >= 0)
    bias = jnp.where(valid, 0.0, -jnp.inf).astype(F32)[:P_CHUNK]
    left = _lane_is_left((P_CHUNK, LANES))
    n_stack = 2 * N_QSLABS

    def attn_seq(s):
        r0 = s * P_CHUNK
        kv_cat = jnp.concatenate([prev[pl.ds(r0, P_CHUNK), ATTN_WIDTH:],
                                  cur[pl.ds(r0, P_CHUNK), ATTN_WIDTH:]], axis=0)
        k_cat = kv_cat[:, :KV_WIDTH]
        v_ext = jnp.concatenate([kv_cat[:, KV_WIDTH:], jnp.ones((2 * P_CHUNK, LANES), BF16)], axis=1)
        qs = []
        for j in range(N_QSLABS):
            q = cur[pl.ds(r0, P_CHUNK), j * LANES:(j + 1) * LANES]
            qs += [jnp.where(left, q, 0), jnp.where(left, 0, q)]
        q_all = jnp.concatenate(qs, axis=0)
        sc = _dot_t(q_all, k_cat).reshape(n_stack, P_CHUNK, 2 * P_CHUNK) + bias[None]
        sc = sc.reshape(n_stack * P_CHUNK, 2 * P_CHUNK)
        sink = sink_ref[...]
        m = jnp.max(jnp.maximum(sc[:, :LANES], sc[:, LANES:]), axis=-1, keepdims=True)
        m = jnp.maximum(jnp.broadcast_to(m, sink.shape), sink)
        p = jnp.concatenate([jnp.exp(sc[:, :LANES] - m), jnp.exp(sc[:, LANES:] - m)], axis=1)
        o_ext = _dot(p.astype(BF16), v_ext)
        o_all = o_ext[:, :LANES] / (o_ext[:, LANES:] + jnp.exp(sink - m))
        for j in range(N_QSLABS):
            o_sc[pl.ds(r0, P_CHUNK), j * LANES:(j + 1) * LANES] = jnp.where(
                left, o_all[2 * j * P_CHUNK:(2 * j + 1) * P_CHUNK],
                o_all[(2 * j + 1) * P_CHUNK:(2 * j + 2) * P_CHUNK])

    for s in range(P_SEQ):
        attn_seq(s)

    sub_rows = SCAN_SUB * P_SEQ

    for c in range(P_CHUNK // SCAN_SUB):
        _s5_block(u_tm, c * sub_rows, sub_rows, bu_sc.at[c % 2], h_ref, lam_r_ref, lam_i_ref,
                  bcat_ref, ccat_ref, dskip_ref, SCAN_SUB, P_SEQ, True)
    y_ssm = jnp.concatenate([u_tm[k] for k in range(N_PARTS)], axis=1)
    ssm_n = _glu_norm(y_ssm, w_glu_ref, b_glu_ref, g_ssm_ref)
    mix_tm = _dot(ssm_n.astype(BF16), wo_s_ref[...])

    attn_n = _rms(o_sc[...], g_attn_ref[...])
    x1_bm = load_x() + _dot(attn_n.astype(BF16), wo_a_ref[...])
    for s in range(P_SEQ):
        for j in range(D_MODEL // LANES):
            xt_sc[j, pl.ds(s, P_CHUNK, stride=P_SEQ), :] = (
                x1_bm[s * P_CHUNK:(s + 1) * P_CHUNK, j * LANES:(j + 1) * LANES])
    x1_ref[...] = jnp.concatenate([xt_sc[j] for j in range(D_MODEL // LANES)], axis=1) + mix_tm


def _ffn_kernel(x1_ref, conv0_ref, g_ffn_ref, w_gate_ref, w_up_ref, conv_w_ref, conv_b_ref,
                w_down_ref, g_final_ref, y_ref, conv_ref, h_sc, *rest, rows_per_step, to_batch_major):
    i = pl.program_id(0)
    n_rows = x1_ref.shape[0]
    halo = 2 * rows_per_step

    @pl.when(i == 0)
    def _():
        conv_ref[...] = conv0_ref[...]

    x1 = x1_ref[...]
    xn = _rms(x1, g_ffn_ref[...]).astype(BF16)
    for sl in range(N_FFN_SLICES):
        c0, c1 = sl * FFN_SLICE, (sl + 1) * FFN_SLICE
        g = _dot(xn, w_gate_ref[:, c0:c1])
        up = _dot(xn, w_up_ref[:, c0:c1])
        gp = jnp.concatenate([conv_ref[:, c0:c1], g], axis=0)
        conv = (conv_b_ref[:, c0:c1]
                + conv_w_ref[0:1, c0:c1] * gp[0:n_rows]
                + conv_w_ref[1:2, c0:c1] * gp[rows_per_step:rows_per_step + n_rows]
                + conv_w_ref[2:3, c0:c1] * gp[halo:halo + n_rows])
        conv_ref[:, c0:c1] = gp[n_rows:n_rows + halo]
        h_sc[:, c0:c1] = (jax.nn.silu(conv) * up).astype(BF16)
    x2 = x1 + _dot(h_sc[...], w_down_ref[...])
    y = _rms(x2, g_final_ref[...])
    if to_batch_major:
        (yt_sc,) = rest
        n_seq = rows_per_step
        n_t = n_rows // n_seq
        for j in range(D_MODEL // LANES):
            yt_sc[j] = y[:, j * LANES:(j + 1) * LANES]
        for s in range(n_seq):
            for j in range(D_MODEL // LANES):
                y_ref[s, :, j * LANES:(j + 1) * LANES] = yt_sc[j, pl.ds(s, n_t, stride=n_seq), :]
    else:
        y_ref[...] = y


def _sample_inproj_kernel(x_ref, h0_re_ref, h0_im_ref, g_mix_ref, w_in_ref, lam_r_ref, lam_i_ref,
                          bcat_ref, ccat_ref, dskip_ref, w_glu_ref, b_glu_ref, g_ssm_ref,
                          q_ref, k_ref, v_ref, ssm_ref, h_re_ref, h_im_ref, u_tm, bu_sc, h_sc,
                          *, n_steps, n_seq):
    n_rows = n_steps * n_seq
    xn = _rms(x_ref[...], g_mix_ref[...]).astype(BF16)
    qkv = _dot(xn, w_in_ref[:, :QKV_COLS])
    q_ref[...] = qkv[:, :ATTN_WIDTH].astype(BF16)
    k_ref[...] = qkv[:, ATTN_WIDTH:ATTN_WIDTH + KV_WIDTH]
    v_ref[...] = qkv[:, ATTN_WIDTH + KV_WIDTH:]
    u = _dot(xn, w_in_ref[:, QKV_COLS:])
    for k in range(N_PARTS):
        u_tm[k] = u[:, k * LANES:(k + 1) * LANES]
    h0_re = h0_re_ref[...].T
    h0_im = h0_im_ref[...].T
    for k in range(N_PARTS):
        h_sc[:, k * 2 * PART_STATE:k * 2 * PART_STATE + PART_STATE] = (
            h0_re[:, k * PART_STATE:(k + 1) * PART_STATE])
        h_sc[:, k * 2 * PART_STATE + PART_STATE:(k + 1) * 2 * PART_STATE] = (
            h0_im[:, k * PART_STATE:(k + 1) * PART_STATE])
    _s5_block(u_tm, 0, n_rows, bu_sc, h_sc, lam_r_ref, lam_i_ref, bcat_ref, ccat_ref,
              dskip_ref, n_steps, n_seq, True)
    h_re_ref[...] = jnp.concatenate(
        [h_sc[:, k * 2 * PART_STATE:k * 2 * PART_STATE + PART_STATE] for k in range(N_PARTS)], axis=1).T
    h_im_ref[...] = jnp.concatenate(
        [h_sc[:, k * 2 * PART_STATE + PART_STATE:(k + 1) * 2 * PART_STATE] for k in range(N_PARTS)],
        axis=1).T
    y_ssm = jnp.concatenate([u_tm[k] for k in range(N_PARTS)], axis=1)
    ssm_ref[...] = _glu_norm(y_ssm, w_glu_ref, b_glu_ref, g_ssm_ref).astype(BF16)


def _sample_attn_kernel(q_ref, kn_ref, vn_ref, kc_ref, vc_ref, sink_ref, x_ref, ssm_ref,
                        g_attn_ref, wo_a_ref, wo_s_ref, x1_ref, o_sc, bias_c_sc, bias_n_sc,
                        *, n_steps, n_seq, seq_batch, n_cache):
    b = pl.program_id(0)
    rows = n_steps * seq_batch
    q_rows = N_QSLABS * rows

    @pl.when(b == 0)
    def _():
        r = lax.broadcasted_iota(jnp.int32, (q_rows, seq_batch * n_cache), 0)
        c = lax.broadcasted_iota(jnp.int32, (q_rows, seq_batch * n_cache), 1)
        t = (r % rows) // seq_batch
        ok = ((r % seq_batch) == (c // n_cache)) & ((c % n_cache) > t)
        bias_c_sc[...] = jnp.where(ok, 0.0, -jnp.inf).astype(F32)
        r = lax.broadcasted_iota(jnp.int32, (q_rows, rows), 0)
        c = lax.broadcasted_iota(jnp.int32, (q_rows, rows), 1)
        t = (r % rows) // seq_batch
        ok = ((r % seq_batch) == (c % seq_batch)) & ((c // seq_batch) <= t)
        bias_n_sc[...] = jnp.where(ok, 0.0, -jnp.inf).astype(F32)

    q = q_ref[...].reshape(rows, ATTN_WIDTH)
    kc = kc_ref[...].astype(BF16).reshape(seq_batch, KV_WIDTH, n_cache)
    vc = vc_ref[...].astype(BF16).reshape(seq_batch, KV_WIDTH, n_cache)
    kc = jnp.concatenate([kc[s] for s in range(seq_batch)], axis=1)
    vc = jnp.concatenate([vc[s] for s in range(seq_batch)], axis=1)
    kn = kn_ref[...].reshape(rows, KV_WIDTH).astype(BF16)
    vn = vn_ref[...].reshape(rows, KV_WIDTH).astype(BF16)
    left = _lane_is_left((rows, LANES))
    halves = []
    for grp in range(N_KV_HEADS):
        qs = []
        for j in range(N_QSLABS):
            qj = q[:, j * LANES:(j + 1) * LANES]
            qs.append(jnp.where(left, qj, 0) if grp == 0 else jnp.where(left, 0, qj))
        qg = jnp.concatenate(qs, axis=0)
        sc_c = _dot(qg, kc) + bias_c_sc[...]
        sc_n = _dot_t(qg, kn) + bias_n_sc[...]
        (p_c, p_n), denom = _softmax_sink([sc_c, sc_n], sink_ref[grp])
        halves.append((_dot_t(p_c.astype(BF16), vc) + _dot(p_n.astype(BF16), vn)) / denom)
    for j in range(N_QSLABS):
        o_j = jnp.where(left, halves[0][j * rows:(j + 1) * rows], halves[1][j * rows:(j + 1) * rows])
        s0 = pl.multiple_of(b * seq_batch, seq_batch)
        o_sc[:, pl.ds(s0, seq_batch), j * LANES:(j + 1) * LANES] = (
            o_j.reshape(n_steps, seq_batch, LANES))

    @pl.when(b == pl.num_programs(0) - 1)
    def _():
        attn_n = _rms(o_sc[...].reshape(n_steps * n_seq, ATTN_WIDTH), g_attn_ref[...])
        x1_ref[...] = (x_ref[...] + _dot(attn_n.astype(BF16), wo_a_ref[...])
                       + _dot(ssm_ref[...], wo_s_ref[...]))


def _full(shape):
    return pl.BlockSpec(shape, lambda *_: (0,) * len(shape))


def _const(shape):
    return pl.BlockSpec(shape, lambda *_: (0,) * len(shape), pipeline_mode=pl.Buffered(1))


def _params(semantics):
    return pltpu.CompilerParams(dimension_semantics=semantics, vmem_limit_bytes=VMEM_LIMIT_BYTES)


def _ffn_call(x1_tm, conv0, w, *, rows_per_step, block_rows, to_batch_major):
    n_rows = x1_tm.shape[0]
    n_blocks = n_rows // block_rows
    halo = 2 * rows_per_step
    in_specs = [
        pl.BlockSpec((block_rows, D_MODEL), lambda i: (i, 0)),
        _const((halo, D_FF)), _const((1, D_MODEL)), _const((D_MODEL, D_FF)), _const((D_MODEL, D_FF)),
        _const((3, D_FF)), _const((1, D_FF)), _const((D_FF, D_MODEL)), _const((1, D_MODEL)),
    ]
    scratch = [pltpu.VMEM((block_rows, D_FF), BF16)]
    if to_batch_major:
        n_t = block_rows // rows_per_step
        y_shape = jax.ShapeDtypeStruct((rows_per_step, (n_blocks - 1) * n_t, D_MODEL), F32)
        y_spec = pl.BlockSpec((rows_per_step, n_t, D_MODEL), lambda i: (0, jnp.maximum(i - 1, 0), 0))
        scratch.append(pltpu.VMEM((D_MODEL // LANES, block_rows, LANES), F32))
    else:
        y_shape = jax.ShapeDtypeStruct((n_rows, D_MODEL), F32)
        y_spec = pl.BlockSpec((block_rows, D_MODEL), lambda i: (i, 0))
    return pl.pallas_call(
        functools.partial(_ffn_kernel, rows_per_step=rows_per_step, to_batch_major=to_batch_major),
        grid=(n_blocks,),
        in_specs=in_specs,
        out_specs=[y_spec, _full((halo, D_FF))],
        out_shape=[y_shape, jax.ShapeDtypeStruct((halo, D_FF), F32)],
        scratch_shapes=scratch,
        compiler_params=_params(("arbitrary",)),
        name="conv_ffn",
    )(x1_tm, conv0, w["g_ffn"], w["w_gate"], w["w_up"], w["conv_w"], w["conv_b"], w["w_down"],
      w["g_final"])


def _pair_heads(x, axis):
    group = N_HEADS // N_KV_HEADS
    shape = x.shape
    x = x.reshape(shape[:axis] + (N_KV_HEADS, group, HEAD_DIM) + shape[axis + 1:])
    x = jnp.swapaxes(x, axis, axis + 1)
    return x.reshape(shape)


def _prep_weights(g_mix, w_in, sinks, lam_re, lam_im, log_dt, b_re, b_im, c_re, c_im, d_skip,
                  w_glu, b_glu, g_attn_out, g_ssm_out, w_o, g_ffn, w_gate, w_up, conv_w, conv_b,
                  w_down, g_final):
    group = N_HEADS // N_KV_HEADS
    w_q = _pair_heads(w_in[:, :ATTN_WIDTH], 1) * (HEAD_DIM ** -0.5)
    w = {
        "g_mix": g_mix.reshape(1, D_MODEL),
        "w_in": jnp.concatenate([w_q, w_in[:, ATTN_WIDTH:]], axis=1).astype(BF16),
        "g_attn": _pair_heads(g_attn_out, 0).reshape(1, ATTN_WIDTH),
        "g_ssm": g_ssm_out.reshape(1, SSM_WIDTH),
        "wo_a": _pair_heads(w_o[:ATTN_WIDTH], 0).astype(BF16),
        "wo_s": w_o[ATTN_WIDTH:].astype(BF16),
        "w_glu": w_glu.astype(BF16),
        "b_glu": b_glu.reshape(1, SSM_WIDTH),
        "dskip": d_skip.reshape(1, SSM_WIDTH),
        "g_ffn": g_ffn.reshape(1, D_MODEL),
        "w_gate": w_gate.astype(BF16),
        "w_up": w_up.astype(BF16),
        "conv_w": conv_w,
        "conv_b": conv_b.reshape(1, D_FF),
        "w_down": w_down.astype(BF16),
        "g_final": g_final.reshape(1, D_MODEL),
    }
    dt = jnp.exp(log_dt)[:, None]
    mag = jnp.exp(lam_re * dt)
    lb_r = mag * jnp.cos(lam_im * dt)
    lb_i = mag * jnp.sin(lam_im * dt)
    inv = 1.0 / (lam_re * lam_re + lam_im * lam_im)
    cf_r = ((lb_r - 1.0) * lam_re + lb_i * lam_im) * inv
    cf_i = (lb_i * lam_re - (lb_r - 1.0) * lam_im) * inv
    bb_r = cf_r[..., None] * b_re - cf_i[..., None] * b_im
    bb_i = cf_r[..., None] * b_im + cf_i[..., None] * b_re
    w["lam_r"] = lb_r.reshape(1, N_SSM_GROUPS * SSM_STATE)
    w["lam_i"] = lb_i.reshape(1, N_SSM_GROUPS * SSM_STATE)
    gpp = LANES // SSM_GROUP
    eye = jnp.eye(gpp, dtype=F32)

    def b_part(b):
        b = b.reshape(N_PARTS, gpp, SSM_STATE, SSM_GROUP).transpose(0, 1, 3, 2)
        b = b[:, :, :, None, :] * eye[None, :, None, :, None]
        return b.reshape(N_PARTS, LANES, PART_STATE)

    def c_part(c):
        c = c.reshape(N_PARTS, gpp, SSM_GROUP, SSM_STATE).transpose(0, 1, 3, 2)
        c = c[:, :, :, None, :] * eye[None, :, None, :, None]
        return c.reshape(N_PARTS, PART_STATE, LANES)

    w["bcat"] = jnp.concatenate([b_part(bb_r), b_part(bb_i)], axis=2).astype(BF16)
    w["ccat"] = jnp.concatenate([c_part(c_re), -c_part(c_im)], axis=1).astype(BF16)
    sink_rows = jnp.repeat(sinks.reshape(N_KV_HEADS, group).T.reshape(N_HEADS), P_CHUNK)
    w["sink_prompt"] = jnp.broadcast_to(sink_rows[:, None], (N_HEADS * P_CHUNK, LANES))
    w["sinks"] = sinks
    return w


def _cols_to_state(h):
    n = h.shape[0]
    st = h.reshape(n, N_PARTS, 2, PART_STATE)
    re = st[:, :, 0].reshape(n, N_SSM_GROUPS, SSM_STATE)
    im = st[:, :, 1].reshape(n, N_SSM_GROUPS, SSM_STATE)
    return re, im


def _prompt_path(x_prompt, meta_tokens, w):
    n_seq, seq_len, _ = x_prompt.shape
    assert n_seq == P_SEQ and seq_len % P_CHUNK == 0
    n_chunks = seq_len // P_CHUNK + 1
    ins = [x_prompt, meta_tokens, w["g_mix"], w["w_in"], w["sink_prompt"], w["lam_r"], w["lam_i"],
           w["bcat"], w["ccat"], w["dskip"], w["w_glu"], w["b_glu"], w["g_attn"], w["g_ssm"],
           w["wo_a"], w["wo_s"]]
    in_specs = [pl.BlockSpec((P_SEQ, P_CHUNK, D_MODEL), lambda i: (0, jnp.maximum(i - 1, 0), 0))]
    in_specs += [_const(a.shape) for a in ins[1:]]
    x1_tm, k_last, v_last, h_last = pl.pallas_call(
        _prompt_mixer_kernel,
        grid=(n_chunks,),
        in_specs=in_specs,
        out_specs=[pl.BlockSpec((P_ROWS, D_MODEL), lambda i: (i, 0)),
                   _full((P_SEQ, P_CHUNK, KV_WIDTH)), _full((P_SEQ, P_CHUNK, KV_WIDTH)),
                   _full((P_SEQ, STATE_COLS))],
        out_shape=[jax.ShapeDtypeStruct((n_chunks * P_ROWS, D_MODEL), F32),
                   jax.ShapeDtypeStruct((P_SEQ, P_CHUNK, KV_WIDTH), F32),
                   jax.ShapeDtypeStruct((P_SEQ, P_CHUNK, KV_WIDTH), F32),
                   jax.ShapeDtypeStruct((P_SEQ, STATE_COLS), F32)],
        scratch_shapes=[
            pltpu.VMEM((2, P_ROWS, QKV_COLS), BF16),
            pltpu.VMEM((N_PARTS, P_ROWS, LANES), F32),
            pltpu.VMEM((2, SCAN_SUB * P_SEQ, STATE_COLS), F32),
            pltpu.VMEM((P_ROWS, ATTN_WIDTH), F32),
            pltpu.VMEM((D_MODEL // LANES, P_ROWS, LANES), F32),
        ],
        compiler_params=_params(("arbitrary",)),
        name="prompt_mixer",
    )(*ins)
    conv0 = jnp.zeros((2 * P_SEQ, D_FF), F32)
    y_prompt, conv_tm = _ffn_call(x1_tm, conv0, w, rows_per_step=P_SEQ, block_rows=P_ROWS,
                                  to_batch_major=True)
    p_re, p_im = _cols_to_state(h_last)
    p_conv = conv_tm.reshape(2, P_SEQ, D_FF).transpose(1, 0, 2)
    kv_shape = (1, P_SEQ, WINDOW, N_KV_HEADS, HEAD_DIM)
    return (y_prompt, k_last.reshape(kv_shape), v_last.reshape(kv_shape), p_re[None], p_im[None],
            p_conv[None])


def _sample_path(x_sample, cache_k, cache_v, st_re, st_im, st_conv, w):
    n_seq, n_steps, _ = x_sample.shape
    n_cache = cache_k.shape[1]
    n_rows = n_steps * n_seq
    seq_batch = 16
    n_state = N_SSM_GROUPS * SSM_STATE
    x_tm = x_sample.transpose(1, 0, 2).reshape(n_rows, D_MODEL)
    h0_re = st_re.transpose(1, 2, 0).reshape(n_state, n_seq)
    h0_im = st_im.transpose(1, 2, 0).reshape(n_state, n_seq)
    ins = [x_tm, h0_re, h0_im, w["g_mix"], w["w_in"], w["lam_r"], w["lam_i"], w["bcat"], w["ccat"],
           w["dskip"], w["w_glu"], w["b_glu"], w["g_ssm"]]
    q, k_new, v_new, ssm_n, h_re, h_im = pl.pallas_call(
        functools.partial(_sample_inproj_kernel, n_steps=n_steps, n_seq=n_seq),
        grid=(1,),
        in_specs=[_full(a.shape) for a in ins],
        out_specs=[_full((n_rows, ATTN_WIDTH)), _full((n_rows, KV_WIDTH)), _full((n_rows, KV_WIDTH)),
                   _full((n_rows, SSM_WIDTH)), _full((n_state, n_seq)), _full((n_state, n_seq))],
        out_shape=[jax.ShapeDtypeStruct((n_rows, ATTN_WIDTH), BF16),
                   jax.ShapeDtypeStruct((n_rows, KV_WIDTH), F32),
                   jax.ShapeDtypeStruct((n_rows, KV_WIDTH), F32),
                   jax.ShapeDtypeStruct((n_rows, SSM_WIDTH), BF16),
                   jax.ShapeDtypeStruct((n_state, n_seq), F32),
                   jax.ShapeDtypeStruct((n_state, n_seq), F32)],
        scratch_shapes=[pltpu.VMEM((N_PARTS, n_rows, LANES), F32),
                        pltpu.VMEM((n_rows, STATE_COLS), F32),
                        pltpu.VMEM((n_seq, STATE_COLS), F32)],
        compiler_params=_params(("arbitrary",)),
        name="sample_inproj",
    )(*ins)

    group = N_HEADS // N_KV_HEADS
    rows = n_steps * seq_batch
    sink_sample = jnp.repeat(w["sinks"].reshape(N_KV_HEADS, group), rows, axis=1)[..., None]
    kc = cache_k.transpose(0, 2, 3, 1)
    vc = cache_v.transpose(0, 2, 3, 1)
    step_block = lambda width: pl.BlockSpec((n_steps, seq_batch, width), lambda b: (0, b, 0))
    cache_block = pl.BlockSpec((seq_batch, N_KV_HEADS, HEAD_DIM, n_cache), lambda b: (b, 0, 0, 0))
    ins = [q.reshape(n_steps, n_seq, ATTN_WIDTH), k_new.reshape(n_steps, n_seq, KV_WIDTH),
           v_new.reshape(n_steps, n_seq, KV_WIDTH), kc, vc, sink_sample, x_tm, ssm_n,
           w["g_attn"], w["wo_a"], w["wo_s"]]
    in_specs = [step_block(ATTN_WIDTH), step_block(KV_WIDTH), step_block(KV_WIDTH), cache_block,
                cache_block] + [_const(a.shape) for a in ins[5:]]
    x1_tm = pl.pallas_call(
        functools.partial(_sample_attn_kernel, n_steps=n_steps, n_seq=n_seq, seq_batch=seq_batch,
                          n_cache=n_cache),
        grid=(n_seq // seq_batch,),
        in_specs=in_specs,
        out_specs=_full((n_rows, D_MODEL)),
        out_shape=jax.ShapeDtypeStruct((n_rows, D_MODEL), F32),
        scratch_shapes=[pltpu.VMEM((n_steps, n_seq, ATTN_WIDTH), F32),
                        pltpu.VMEM((N_QSLABS * rows, seq_batch * n_cache), F32),
                        pltpu.VMEM((N_QSLABS * rows, rows), F32)],
        compiler_params=_params(("arbitrary",)),
        name="sample_attn",
    )(*ins)

    conv0 = st_conv.transpose(1, 0, 2).reshape(2 * n_seq, D_FF)
    y_tm, conv_tm = _ffn_call(x1_tm, conv0, w, rows_per_step=n_seq, block_rows=n_rows,
                              to_batch_major=False)
    y_sample = y_tm.reshape(n_steps, n_seq, D_MODEL).transpose(1, 0, 2)
    s_conv = conv_tm.reshape(2, n_seq, D_FF).transpose(1, 0, 2)
    s_re = h_re.reshape(N_SSM_GROUPS, SSM_STATE, n_seq).transpose(2, 0, 1)
    s_im = h_im.reshape(N_SSM_GROUPS, SSM_STATE, n_seq).transpose(2, 0, 1)

    def window(cache_t, new):
        new_t = new.reshape(n_steps, n_seq, N_KV_HEADS, HEAD_DIM).transpose(1, 2, 3, 0)
        return jnp.concatenate([cache_t[..., n_steps:], new_t], axis=-1).transpose(0, 3, 1, 2)

    return (y_sample, window(kc, k_new)[None], window(vc, v_new)[None], s_re[None], s_im[None],
            s_conv[None])


def kernel(x_prompt, x_sample, cache_k_win, cache_v_win, state_ssm_re, state_ssm_im, state_conv, meta_tokens, g_mix, w_in, sinks, lam_re, lam_im, log_dt, b_re, b_im, c_re, c_im, d_skip, w_glu, b_glu, g_attn_out, g_ssm_out, w_o, g_ffn, w_gate, w_up, conv_w, conv_b, w_down, g_final):
    assert g_mix.shape[0] == 1, "single-layer kernel"
    w = _prep_weights(g_mix[0], w_in[0], sinks[0], lam_re[0], lam_im[0], log_dt[0], b_re[0], b_im[0],
                      c_re[0], c_im[0], d_skip[0], w_glu[0], b_glu[0], g_attn_out[0], g_ssm_out[0],
                      w_o[0], g_ffn[0], w_gate[0], w_up[0], conv_w[0], conv_b[0], w_down[0], g_final)
    y_p, p_k, p_v, p_re, p_im, p_conv = _prompt_path(x_prompt, meta_tokens, w)
    y_s, s_k, s_v, s_re, s_im, s_conv = _sample_path(
        x_sample, cache_k_win[0], cache_v_win[0], state_ssm_re[0], state_ssm_im[0], state_conv[0], w)
    return (y_p, y_s, p_k, p_v, p_re, p_im, p_conv, s_k, s_v, s_re, s_im, s_conv)
```

```python
import functools
import math

import jax
import jax.numpy as jnp
from jax import lax
from jax.experimental import pallas as pl
from jax.experimental.pallas import tpu as pltpu

F32 = jnp.float32
BF16 = jnp.bfloat16

D_MODEL = 1024
N_META = 16
HEAD_DIM = 64
ATTN_WIDTH = 512
N_HEADS = 8
N_KV_HEADS = 2
KV_WIDTH = 128
WINDOW = 128
SSM_WIDTH = 512
SSM_GROUP = 16
N_SSM_GROUPS = 32
SSM_STATE = 64
D_FF = 2816
EPS = 1e-5

LANES = 128
SUBLANES = 8
N_QSLABS = ATTN_WIDTH // LANES
N_PARTS = SSM_WIDTH // LANES
PART_STATE = (LANES // SSM_GROUP) * SSM_STATE
STATE_COLS = N_PARTS * 2 * PART_STATE
FFN_SLICE = 256
N_FFN_SLICES = D_FF // FFN_SLICE
QKV_COLS = ATTN_WIDTH + 2 * KV_WIDTH
VMEM_LIMIT_BYTES = 58 * 1024 * 1024

P_CHUNK = 128
P_SEQ = 8
P_ROWS = P_CHUNK * P_SEQ
P_FRONT = P_CHUNK - N_META
SCAN_SUB = 32


def _rms(x, g):
    return x * lax.rsqrt(jnp.mean(x * x, axis=-1, keepdims=True) + EPS) * g


def _dot(a, b):
    return jnp.dot(a, b, preferred_element_type=F32)


def _dot_t(a, b):
    return lax.dot_general(a, b, (((1,), (1,)), ((), ())), preferred_element_type=F32)


def _lane_is_left(shape):
    return lax.broadcasted_iota(jnp.int32, shape, len(shape) - 1) < HEAD_DIM


def _s5_scan(bu_ref, h_ref, lam_r_ref, lam_i_ref, n_steps, rows_per_step, unroll):
    n_tiles = rows_per_step // SUBLANES

    def tile_body(r, carry):
        r0 = pl.multiple_of(r * SUBLANES, SUBLANES)
        for k in range(N_PARTS):
            c_re = k * 2 * PART_STATE
            c_im = c_re + PART_STATE
            lr = jnp.broadcast_to(lam_r_ref[:, k * PART_STATE:(k + 1) * PART_STATE],
                                  (SUBLANES, PART_STATE))
            li = jnp.broadcast_to(lam_i_ref[:, k * PART_STATE:(k + 1) * PART_STATE],
                                  (SUBLANES, PART_STATE))
            hr = h_ref[pl.ds(r0, SUBLANES), c_re:c_re + PART_STATE]
            hi = h_ref[pl.ds(r0, SUBLANES), c_im:c_im + PART_STATE]

            def step(t, h):
                hr, hi = h
                row = pl.multiple_of(t * rows_per_step + r0, SUBLANES)
                br = bu_ref[pl.ds(row, SUBLANES), c_re:c_re + PART_STATE]
                bi = bu_ref[pl.ds(row, SUBLANES), c_im:c_im + PART_STATE]
                nr = lr * hr - li * hi + br
                ni = lr * hi + li * hr + bi
                bu_ref[pl.ds(row, SUBLANES), c_re:c_re + PART_STATE] = nr
                bu_ref[pl.ds(row, SUBLANES), c_im:c_im + PART_STATE] = ni
                return nr, ni

            hr, hi = lax.fori_loop(0, n_steps, step, (hr, hi), unroll=unroll)
            h_ref[pl.ds(r0, SUBLANES), c_re:c_re + PART_STATE] = hr
            h_ref[pl.ds(r0, SUBLANES), c_im:c_im + PART_STATE] = hi
        return carry

    if n_tiles == 1:
        tile_body(0, 0)
    else:
        lax.fori_loop(0, n_tiles, tile_body, 0)


def _s5_block(u_tm_ref, row0, n_rows, bu_ref, h_ref, lam_r_ref, lam_i_ref,
              bcat_ref, ccat_ref, dskip_ref, n_steps, rows_per_step, unroll):
    for k in range(N_PARTS):
        u_k = u_tm_ref[k, pl.ds(row0, n_rows), :].astype(BF16)
        bu_ref[:, k * 2 * PART_STATE:(k + 1) * 2 * PART_STATE] = _dot(u_k, bcat_ref[k])
    _s5_scan(bu_ref, h_ref, lam_r_ref, lam_i_ref, n_steps, rows_per_step, unroll)
    for k in range(N_PARTS):
        h_k = bu_ref[:, k * 2 * PART_STATE:(k + 1) * 2 * PART_STATE].astype(BF16)
        y_k = _dot(h_k, ccat_ref[k])
        u_k = u_tm_ref[k, pl.ds(row0, n_rows), :]
        u_tm_ref[k, pl.ds(row0, n_rows), :] = y_k + dskip_ref[:, k * LANES:(k + 1) * LANES] * u_k


def _glu_norm(y, w_glu_ref, b_glu_ref, g_ssm_ref):
    g = jax.nn.gelu(y)
    gate = jax.nn.sigmoid(_dot(g.astype(BF16), w_glu_ref[...]) + b_glu_ref[...])
    return _rms(g * gate, g_ssm_ref[...])


def _softmax_sink(parts, sink):
    m = sink
    for s in parts:
        m = jnp.maximum(m, jnp.max(s, axis=-1, keepdims=True))
    ps = [jnp.exp(s - m) for s in parts]
    denom = jnp.exp(sink - m)
    for p in ps:
        denom = denom + jnp.sum(p, axis=-1, keepdims=True)
    return ps, denom


def _prompt_mixer_kernel(x_ref, meta_ref, g_mix_ref, w_in_ref, sink_ref, lam_r_ref, lam_i_ref,
                         bcat_ref, ccat_ref, dskip_ref, w_glu_ref, b_glu_ref, g_attn_ref,
                         g_ssm_ref, wo_a_ref, wo_s_ref,
                         x1_ref, k_out_ref, v_out_ref, h_ref,
                         qkv_sc, u_tm, bu_sc, o_sc, xt_sc):
    i = pl.program_id(0)
    cur = qkv_sc.at[i % 2]
    prev = qkv_sc.at[(i + 1) % 2]

    @pl.when(i == 0)
    def _():
        h_ref[...] = jnp.zeros_like(h_ref)
        qkv_sc[1] = jnp.zeros(qkv_sc.shape[1:], BF16)

    def load_x():
        first = jnp.concatenate([jnp.zeros((P_FRONT, D_MODEL), F32), meta_ref[...]], axis=0)
        first = jnp.broadcast_to(first[None], (P_SEQ, P_CHUNK, D_MODEL))
        return jnp.where(i == 0, first, x_ref[...]).reshape(P_ROWS, D_MODEL)

    xn = _rms(load_x(), g_mix_ref[...]).astype(BF16)
    qkv = _dot(xn, w_in_ref[:, :QKV_COLS])
    k_out_ref[...] = qkv[:, ATTN_WIDTH:ATTN_WIDTH + KV_WIDTH].reshape(P_SEQ, P_CHUNK, KV_WIDTH)
    v_out_ref[...] = qkv[:, ATTN_WIDTH + KV_WIDTH:].reshape(P_SEQ, P_CHUNK, KV_WIDTH)
    cur[...] = qkv.astype(BF16)
    u = _dot(xn, w_in_ref[:, QKV_COLS:])
    for s in range(P_SEQ):
        for k in range(N_PARTS):
            u_tm[k, pl.ds(s, P_CHUNK, stride=P_SEQ), :] = (
                u[s * P_CHUNK:(s + 1) * P_CHUNK, k * LANES:(k + 1) * LANES])

    tq = lax.broadcasted_iota(jnp.int32, (2 * P_CHUNK, 2 * P_CHUNK), 0) % P_CHUNK
    col = lax.broadcasted_iota(jnp.int32, (2 * P_CHUNK, 2 * P_CHUNK), 1)
    is_prev = col < P_CHUNK
    tk = jnp.where(is_prev, col, col - P_CHUNK)
    kpos = jnp.where(is_prev, (i - 1) * P_CHUNK, i * P_CHUNK) + tk - P_FRONT
    valid = ((is_prev & (tk > tq)) | (jnp.logical_not(is_prev) & (tk <= tq))) & (kpos >= 0)
    bias = jnp.where(valid, 0.0, -jnp.inf).astype(F32)[:P_CHUNK]
    left = _lane_is_left((P_CHUNK, LANES))
    n_stack = 2 * N_QSLABS

    def attn_seq(s):
        r0 = s * P_CHUNK
        kv_cat = jnp.concatenate([prev[pl.ds(r0, P_CHUNK), ATTN_WIDTH:],
                                  cur[pl.ds(r0, P_CHUNK), ATTN_WIDTH:]], axis=0)
        k_cat = kv_cat[:, :KV_WIDTH]
        v_ext = jnp.concatenate([kv_cat[:, KV_WIDTH:], jnp.ones((2 * P_CHUNK, LANES), BF16)], axis=1)
        qs = []
        for j in range(N_QSLABS):
            q = cur[pl.ds(r0, P_CHUNK), j * LANES:(j + 1) * LANES]
            qs += [jnp.where(left, q, 0), jnp.where(left, 0, q)]
        q_all = jnp.concatenate(qs, axis=0)
        sc = _dot_t(q_all, k_cat).reshape(n_stack, P_CHUNK, 2 * P_CHUNK) + bias[None]
        sc = sc.reshape(n_stack * P_CHUNK, 2 * P_CHUNK)
        sink = sink_ref[...]
        m = jnp.max(jnp.maximum(sc[:, :LANES], sc[:, LANES:]), axis=-1, keepdims=True)
        m = jnp.maximum(jnp.broadcast_to(m, sink.shape), sink)
        p = jnp.concatenate([jnp.exp(sc[:, :LANES] - m), jnp.exp(sc[:, LANES:] - m)], axis=1)
        o_ext = _dot(p.astype(BF16), v_ext)
        o_all = o_ext[:, :LANES] / (o_ext[:, LANES:] + jnp.exp(sink - m))
        for j in range(N_QSLABS):
            o_sc[pl.ds(r0, P_CHUNK), j * LANES:(j + 1) * LANES] = jnp.where(
                left, o_all[2 * j * P_CHUNK:(2 * j + 1) * P_CHUNK],
                o_all[(2 * j + 1) * P_CHUNK:(2 * j + 2) * P_CHUNK])

    for s in range(P_SEQ):
        attn_seq(s)

    sub_rows = SCAN_SUB * P_SEQ

    for c in range(P_CHUNK // SCAN_SUB):
        _s5_block(u_tm, c * sub_rows, sub_rows, bu_sc.at[c % 2], h_ref, lam_r_ref, lam_i_ref,
                  bcat_ref, ccat_ref, dskip_ref, SCAN_SUB, P_SEQ, True)
    y_ssm = jnp.concatenate([u_tm[k] for k in range(N_PARTS)], axis=1)
    ssm_n = _glu_norm(y_ssm, w_glu_ref, b_glu_ref, g_ssm_ref)
    mix_tm = _dot(ssm_n.astype(BF16), wo_s_ref[...])

    attn_n = _rms(o_sc[...], g_attn_ref[...])
    x1_bm = load_x() + _dot(attn_n.astype(BF16), wo_a_ref[...])
    for s in range(P_SEQ):
        for j in range(D_MODEL // LANES):
            xt_sc[j, pl.ds(s, P_CHUNK, stride=P_SEQ), :] = (
                x1_bm[s * P_CHUNK:(s + 1) * P_CHUNK, j * LANES:(j + 1) * LANES])
    x1_ref[...] = jnp.concatenate([xt_sc[j] for j in range(D_MODEL // LANES)], axis=1) + mix_tm


def _ffn_kernel(x1_ref, g_ffn_ref, w_gate_ref, w_up_ref, conv_w_ref, conv_b_ref, w_down_ref,
                g_final_ref, *rest, rows_per_step, prompt):
    i = pl.program_id(0)
    n_rows = x1_ref.shape[0]
    halo = 2 * rows_per_step
    if prompt:
        y_ref, conv_ref, h_sc, yt_sc = rest
    else:
        conv0_ref, y_ref, conv_ref, h_sc = rest

    def full_block():
        x1 = x1_ref[...]
        xn = _rms(x1, g_ffn_ref[...]).astype(BF16)
        for sl in range(N_FFN_SLICES):
            c0, c1 = sl * FFN_SLICE, (sl + 1) * FFN_SLICE
            g = _dot(xn, w_gate_ref[:, c0:c1])
            up = _dot(xn, w_up_ref[:, c0:c1])
            gp = jnp.concatenate([conv_ref[:, c0:c1], g], axis=0)
            conv = (conv_b_ref[:, c0:c1]
                    + conv_w_ref[0:1, c0:c1] * gp[0:n_rows]
                    + conv_w_ref[1:2, c0:c1] * gp[rows_per_step:rows_per_step + n_rows]
                    + conv_w_ref[2:3, c0:c1] * gp[halo:halo + n_rows])
            conv_ref[:, c0:c1] = gp[n_rows:n_rows + halo]
            h_sc[:, c0:c1] = (jax.nn.silu(conv) * up).astype(BF16)
        x2 = x1 + _dot(h_sc[...], w_down_ref[...])
        return _rms(x2, g_final_ref[...])

    if prompt:
        @pl.when(i == 0)
        def _():
            tail = _rms(x1_ref[n_rows - halo:, :], g_ffn_ref[...]).astype(BF16)
            conv_ref[...] = _dot(tail, w_gate_ref[...])

        @pl.when(i > 0)
        def _():
            y = full_block()
            n_seq = rows_per_step
            n_t = n_rows // n_seq
            for j in range(D_MODEL // LANES):
                yt_sc[j] = y[:, j * LANES:(j + 1) * LANES]
            for s in range(n_seq):
                for j in range(D_MODEL // LANES):
                    y_ref[s, :, j * LANES:(j + 1) * LANES] = yt_sc[j, pl.ds(s, n_t, stride=n_seq), :]
    else:
        @pl.when(i == 0)
        def _():
            conv_ref[...] = conv0_ref[...]

        y_ref[...] = full_block()


def _sample_inproj_kernel(x_ref, h0_re_ref, h0_im_ref, g_mix_ref, w_in_ref, lam_r_ref, lam_i_ref,
                          bcat_ref, ccat_ref, dskip_ref, w_glu_ref, b_glu_ref, g_ssm_ref,
                          q_ref, k_ref, v_ref, kt_ref, vt_ref, ssm_ref, h_re_ref, h_im_ref,
                          u_tm, bu_sc, h_sc, st_sc, *, n_steps, n_seq):
    n_rows = n_steps * n_seq
    xn = _rms(x_ref[...], g_mix_ref[...]).astype(BF16)
    qkv = _dot(xn, w_in_ref[:, :QKV_COLS])
    q_ref[...] = qkv[:, :ATTN_WIDTH].astype(BF16)
    k_ref[...] = qkv[:, ATTN_WIDTH:ATTN_WIDTH + KV_WIDTH]
    v_ref[...] = qkv[:, ATTN_WIDTH + KV_WIDTH:]
    for i, (out_t, c0) in enumerate(((kt_ref, ATTN_WIDTH), (vt_ref, ATTN_WIDTH + KV_WIDTH))):
        for t in range(n_steps):
            st_sc[i, pl.ds(t, n_seq, stride=n_steps), :] = (
                qkv[t * n_seq:(t + 1) * n_seq, c0:c0 + KV_WIDTH])
        out_t[...] = st_sc[i].T
    u = _dot(xn, w_in_ref[:, QKV_COLS:])
    for k in range(N_PARTS):
        u_tm[k] = u[:, k * LANES:(k + 1) * LANES]
    h0_re = h0_re_ref[...].T
    h0_im = h0_im_ref[...].T
    for k in range(N_PARTS):
        h_sc[:, k * 2 * PART_STATE:k * 2 * PART_STATE + PART_STATE] = (
            h0_re[:, k * PART_STATE:(k + 1) * PART_STATE])
        h_sc[:, k * 2 * PART_STATE + PART_STATE:(k + 1) * 2 * PART_STATE] = (
            h0_im[:, k * PART_STATE:(k + 1) * PART_STATE])
    _s5_block(u_tm, 0, n_rows, bu_sc, h_sc, lam_r_ref, lam_i_ref, bcat_ref, ccat_ref,
              dskip_ref, n_steps, n_seq, True)
    h_re_ref[...] = jnp.concatenate(
        [h_sc[:, k * 2 * PART_STATE:k * 2 * PART_STATE + PART_STATE] for k in range(N_PARTS)], axis=1).T
    h_im_ref[...] = jnp.concatenate(
        [h_sc[:, k * 2 * PART_STATE + PART_STATE:(k + 1) * 2 * PART_STATE] for k in range(N_PARTS)],
        axis=1).T
    y_ssm = jnp.concatenate([u_tm[k] for k in range(N_PARTS)], axis=1)
    ssm_ref[...] = _glu_norm(y_ssm, w_glu_ref, b_glu_ref, g_ssm_ref).astype(BF16)


def _sample_attn_kernel(q_ref, kn_ref, vn_ref, knt_ref, vnt_ref, kc_ref, vc_ref, sink_ref, x_ref,
                        ssm_ref, g_attn_ref, wo_a_ref, wo_s_ref, x1_ref, kw_ref, vw_ref,
                        o_sc, bias_c_sc, bias_n_sc,
                        *, n_steps, n_seq, seq_batch, n_cache):
    b = pl.program_id(0)
    rows = n_steps * seq_batch
    q_rows = N_QSLABS * rows

    @pl.when(b == 0)
    def _():
        r = lax.broadcasted_iota(jnp.int32, (q_rows, seq_batch * n_cache), 0)
        c = lax.broadcasted_iota(jnp.int32, (q_rows, seq_batch * n_cache), 1)
        t = (r % rows) // seq_batch
        ok = ((r % seq_batch) == (c // n_cache)) & ((c % n_cache) > t)
        bias_c_sc[...] = jnp.where(ok, 0.0, -jnp.inf).astype(F32)
        r = lax.broadcasted_iota(jnp.int32, (q_rows, rows), 0)
        c = lax.broadcasted_iota(jnp.int32, (q_rows, rows), 1)
        t = (r % rows) // seq_batch
        ok = ((r % seq_batch) == (c % seq_batch)) & ((c // seq_batch) <= t)
        bias_n_sc[...] = jnp.where(ok, 0.0, -jnp.inf).astype(F32)

    is_new = lax.broadcasted_iota(jnp.int32, (KV_WIDTH, n_cache), 1) >= n_cache - n_steps
    for cache_ref, new_ref, win_ref in ((kc_ref, knt_ref, kw_ref), (vc_ref, vnt_ref, vw_ref)):
        old = pltpu.roll(cache_ref[...].reshape(seq_batch * KV_WIDTH, n_cache), n_cache - n_steps, 1)
        new = new_ref[0]
        new = jnp.concatenate([new, jnp.zeros((KV_WIDTH, n_cache - new.shape[1]), F32)], axis=1)
        for s in range(seq_batch):
            put = pltpu.roll(new, n_cache - n_steps * (s + 1), 1)
            win_ref[s] = jnp.where(is_new, put, old[s * KV_WIDTH:(s + 1) * KV_WIDTH]).reshape(
                N_KV_HEADS, HEAD_DIM, n_cache)

    q = q_ref[...].reshape(rows, ATTN_WIDTH)
    kc = kc_ref[...].astype(BF16).reshape(seq_batch, KV_WIDTH, n_cache)
    vc = vc_ref[...].astype(BF16).reshape(seq_batch, KV_WIDTH, n_cache)
    kc = jnp.concatenate([kc[s] for s in range(seq_batch)], axis=1)
    vc = jnp.concatenate([vc[s] for s in range(seq_batch)], axis=1)
    kn = kn_ref[...].reshape(rows, KV_WIDTH).astype(BF16)
    vn = vn_ref[...].reshape(rows, KV_WIDTH).astype(BF16)
    left = _lane_is_left((rows, LANES))
    halves = []
    for grp in range(N_KV_HEADS):
        qs = []
        for j in range(N_QSLABS):
            qj = q[:, j * LANES:(j + 1) * LANES]
            qs.append(jnp.where(left, qj, 0) if grp == 0 else jnp.where(left, 0, qj))
        qg = jnp.concatenate(qs, axis=0)
        sc_c = _dot(qg, kc) + bias_c_sc[...]
        sc_n = _dot_t(qg, kn) + bias_n_sc[...]
        (p_c, p_n), denom = _softmax_sink([sc_c, sc_n], sink_ref[grp])
        halves.append((_dot_t(p_c.astype(BF16), vc) + _dot(p_n.astype(BF16), vn)) / denom)
    for j in range(N_QSLABS):
        o_j = jnp.where(left, halves[0][j * rows:(j + 1) * rows], halves[1][j * rows:(j + 1) * rows])
        s0 = pl.multiple_of(b * seq_batch, seq_batch)
        o_sc[:, pl.ds(s0, seq_batch), j * LANES:(j + 1) * LANES] = (
            o_j.reshape(n_steps, seq_batch, LANES))

    @pl.when(b == pl.num_programs(0) - 1)
    def _():
        attn_n = _rms(o_sc[...].reshape(n_steps * n_seq, ATTN_WIDTH), g_attn_ref[...])
        x1_ref[...] = (x_ref[...] + _dot(attn_n.astype(BF16), wo_a_ref[...])
                       + _dot(ssm_ref[...], wo_s_ref[...]))


def _full(shape):
    return pl.BlockSpec(shape, lambda *_: (0,) * len(shape))


def _const(shape):
    return pl.BlockSpec(shape, lambda *_: (0,) * len(shape), pipeline_mode=pl.Buffered(1))


def _params(semantics):
    return pltpu.CompilerParams(dimension_semantics=semantics, vmem_limit_bytes=VMEM_LIMIT_BYTES)


def _ffn_call(x1_tm, conv0, w, *, rows_per_step, block_rows):
    prompt = conv0 is None
    n_rows = x1_tm.shape[0]
    n_blocks = n_rows // block_rows
    halo = 2 * rows_per_step
    ins = [x1_tm, w["g_ffn"], w["w_gate"], w["w_up"], w["conv_w"], w["conv_b"], w["w_down"],
           w["g_final"]]
    in_specs = [pl.BlockSpec((block_rows, D_MODEL), lambda i: (i, 0))]
    in_specs += [_const(a.shape) for a in ins[1:]]
    scratch = [pltpu.VMEM((block_rows, D_FF), BF16)]
    if prompt:
        n_t = block_rows // rows_per_step
        y_shape = jax.ShapeDtypeStruct((rows_per_step, (n_blocks - 1) * n_t, D_MODEL), F32)
        y_spec = pl.BlockSpec((rows_per_step, n_t, D_MODEL), lambda i: (0, jnp.maximum(i - 1, 0), 0))
        scratch.append(pltpu.VMEM((D_MODEL // LANES, block_rows, LANES), F32))
    else:
        ins.append(conv0)
        in_specs.append(_const((halo, D_FF)))
        y_shape = jax.ShapeDtypeStruct((n_rows, D_MODEL), F32)
        y_spec = pl.BlockSpec((block_rows, D_MODEL), lambda i: (i, 0))
    return pl.pallas_call(
        functools.partial(_ffn_kernel, rows_per_step=rows_per_step, prompt=prompt),
        grid=(n_blocks,),
        in_specs=in_specs,
        out_specs=[y_spec, _full((halo, D_FF))],
        out_shape=[y_shape, jax.ShapeDtypeStruct((halo, D_FF), F32)],
        scratch_shapes=scratch,
        compiler_params=_params(("arbitrary",)),
        name="conv_ffn",
    )(*ins)


def _pair_heads(x, axis):
    group = N_HEADS // N_KV_HEADS
    shape = x.shape
    x = x.reshape(shape[:axis] + (N_KV_HEADS, group, HEAD_DIM) + shape[axis + 1:])
    x = jnp.swapaxes(x, axis, axis + 1)
    return x.reshape(shape)


def _prep_weights(g_mix, w_in, sinks, lam_re, lam_im, log_dt, b_re, b_im, c_re, c_im, d_skip,
                  w_glu, b_glu, g_attn_out, g_ssm_out, w_o, g_ffn, w_gate, w_up, conv_w, conv_b,
                  w_down, g_final):
    group = N_HEADS // N_KV_HEADS
    w_q = _pair_heads(w_in[:, :ATTN_WIDTH], 1) * (HEAD_DIM ** -0.5)
    w = {
        "g_mix": g_mix.reshape(1, D_MODEL),
        "w_in": jnp.concatenate([w_q, w_in[:, ATTN_WIDTH:]], axis=1).astype(BF16),
        "g_attn": _pair_heads(g_attn_out, 0).reshape(1, ATTN_WIDTH),
        "g_ssm": g_ssm_out.reshape(1, SSM_WIDTH),
        "wo_a": _pair_heads(w_o[:ATTN_WIDTH], 0).astype(BF16),
        "wo_s": w_o[ATTN_WIDTH:].astype(BF16),
        "w_glu": w_glu.astype(BF16),
        "b_glu": b_glu.reshape(1, SSM_WIDTH),
        "dskip": d_skip.reshape(1, SSM_WIDTH),
        "g_ffn": g_ffn.reshape(1, D_MODEL),
        "w_gate": w_gate.astype(BF16),
        "w_up": w_up.astype(BF16),
        "conv_w": conv_w,
        "conv_b": conv_b.reshape(1, D_FF),
        "w_down": w_down.astype(BF16),
        "g_final": g_final.reshape(1, D_MODEL),
    }
    dt = jnp.exp(log_dt)[:, None]
    mag = jnp.exp(lam_re * dt)
    lb_r = mag * jnp.cos(lam_im * dt)
    lb_i = mag * jnp.sin(lam_im * dt)
    inv = 1.0 / (lam_re * lam_re + lam_im * lam_im)
    cf_r = ((lb_r - 1.0) * lam_re + lb_i * lam_im) * inv
    cf_i = (lb_i * lam_re - (lb_r - 1.0) * lam_im) * inv
    bb_r = cf_r[..., None] * b_re - cf_i[..., None] * b_im
    bb_i = cf_r[..., None] * b_im + cf_i[..., None] * b_re
    w["lam_r"] = lb_r.reshape(1, N_SSM_GROUPS * SSM_STATE)
    w["lam_i"] = lb_i.reshape(1, N_SSM_GROUPS * SSM_STATE)
    gpp = LANES // SSM_GROUP
    eye = jnp.eye(gpp, dtype=F32)

    def b_part(b):
        b = b.reshape(N_PARTS, gpp, SSM_STATE, SSM_GROUP).transpose(0, 1, 3, 2)
        b = b[:, :, :, None, :] * eye[None, :, None, :, None]
        return b.reshape(N_PARTS, LANES, PART_STATE)

    def c_part(c):
        c = c.reshape(N_PARTS, gpp, SSM_GROUP, SSM_STATE).transpose(0, 1, 3, 2)
        c = c[:, :, :, None, :] * eye[None, :, None, :, None]
        return c.reshape(N_PARTS, PART_STATE, LANES)

    w["bcat"] = jnp.concatenate([b_part(bb_r), b_part(bb_i)], axis=2).astype(BF16)
    w["ccat"] = jnp.concatenate([c_part(c_re), -c_part(c_im)], axis=1).astype(BF16)
    sink_rows = jnp.repeat(sinks.reshape(N_KV_HEADS, group).T.reshape(N_HEADS), P_CHUNK)
    w["sink_prompt"] = jnp.broadcast_to(sink_rows[:, None], (N_HEADS * P_CHUNK, LANES))
    w["sinks"] = sinks
    return w


def _cols_to_state(h):
    n = h.shape[0]
    st = h.reshape(n, N_PARTS, 2, PART_STATE)
    re = st[:, :, 0].reshape(n, N_SSM_GROUPS, SSM_STATE)
    im = st[:, :, 1].reshape(n, N_SSM_GROUPS, SSM_STATE)
    return re, im


def _prompt_path(x_prompt, meta_tokens, w):
    n_seq, seq_len, _ = x_prompt.shape
    assert n_seq == P_SEQ and seq_len % P_CHUNK == 0
    n_chunks = seq_len // P_CHUNK + 1
    ins = [x_prompt, meta_tokens, w["g_mix"], w["w_in"], w["sink_prompt"], w["lam_r"], w["lam_i"],
           w["bcat"], w["ccat"], w["dskip"], w["w_glu"], w["b_glu"], w["g_attn"], w["g_ssm"],
           w["wo_a"], w["wo_s"]]
    in_specs = [pl.BlockSpec((P_SEQ, P_CHUNK, D_MODEL), lambda i: (0, jnp.maximum(i - 1, 0), 0))]
    in_specs += [_const(a.shape) for a in ins[1:]]
    x1_tm, k_last, v_last, h_last = pl.pallas_call(
        _prompt_mixer_kernel,
        grid=(n_chunks,),
        in_specs=in_specs,
        out_specs=[pl.BlockSpec((P_ROWS, D_MODEL), lambda i: (i, 0)),
                   _full((P_SEQ, P_CHUNK, KV_WIDTH)), _full((P_SEQ, P_CHUNK, KV_WIDTH)),
                   _full((P_SEQ, STATE_COLS))],
        out_shape=[jax.ShapeDtypeStruct((n_chunks * P_ROWS, D_MODEL), F32),
                   jax.ShapeDtypeStruct((P_SEQ, P_CHUNK, KV_WIDTH), F32),
                   jax.ShapeDtypeStruct((P_SEQ, P_CHUNK, KV_WIDTH), F32),
                   jax.ShapeDtypeStruct((P_SEQ, STATE_COLS), F32)],
        scratch_shapes=[
            pltpu.VMEM((2, P_ROWS, QKV_COLS), BF16),
            pltpu.VMEM((N_PARTS, P_ROWS, LANES), F32),
            pltpu.VMEM((2, SCAN_SUB * P_SEQ, STATE_COLS), F32),
            pltpu.VMEM((P_ROWS, ATTN_WIDTH), F32),
            pltpu.VMEM((D_MODEL // LANES, P_ROWS, LANES), F32),
        ],
        compiler_params=_params(("arbitrary",)),
        name="prompt_mixer",
    )(*ins)
    y_prompt, conv_tm = _ffn_call(x1_tm, None, w, rows_per_step=P_SEQ, block_rows=P_ROWS)
    p_re, p_im = _cols_to_state(h_last)
    p_conv = conv_tm.reshape(2, P_SEQ, D_FF).transpose(1, 0, 2)
    kv_shape = (1, P_SEQ, WINDOW, N_KV_HEADS, HEAD_DIM)
    return (y_prompt, k_last.reshape(kv_shape), v_last.reshape(kv_shape), p_re[None], p_im[None],
            p_conv[None])


def _sample_path(x_sample, cache_k, cache_v, st_re, st_im, st_conv, w):
    n_seq, n_steps, _ = x_sample.shape
    n_cache = cache_k.shape[1]
    n_rows = n_steps * n_seq
    seq_batch = 16
    n_state = N_SSM_GROUPS * SSM_STATE
    x_tm = x_sample.transpose(1, 0, 2).reshape(n_rows, D_MODEL)
    h0_re = st_re.transpose(1, 2, 0).reshape(n_state, n_seq)
    h0_im = st_im.transpose(1, 2, 0).reshape(n_state, n_seq)
    ins = [x_tm, h0_re, h0_im, w["g_mix"], w["w_in"], w["lam_r"], w["lam_i"], w["bcat"], w["ccat"],
           w["dskip"], w["w_glu"], w["b_glu"], w["g_ssm"]]
    q, k_new, v_new, k_new_t, v_new_t, ssm_n, h_re, h_im = pl.pallas_call(
        functools.partial(_sample_inproj_kernel, n_steps=n_steps, n_seq=n_seq),
        grid=(1,),
        in_specs=[_full(a.shape) for a in ins],
        out_specs=[_full((n_rows, ATTN_WIDTH)), _full((n_rows, KV_WIDTH)), _full((n_rows, KV_WIDTH)),
                   _full((KV_WIDTH, n_rows)), _full((KV_WIDTH, n_rows)),
                   _full((n_rows, SSM_WIDTH)), _full((n_state, n_seq)), _full((n_state, n_seq))],
        out_shape=[jax.ShapeDtypeStruct((n_rows, ATTN_WIDTH), BF16),
                   jax.ShapeDtypeStruct((n_rows, KV_WIDTH), F32),
                   jax.ShapeDtypeStruct((n_rows, KV_WIDTH), F32),
                   jax.ShapeDtypeStruct((KV_WIDTH, n_rows), F32),
                   jax.ShapeDtypeStruct((KV_WIDTH, n_rows), F32),
                   jax.ShapeDtypeStruct((n_rows, SSM_WIDTH), BF16),
                   jax.ShapeDtypeStruct((n_state, n_seq), F32),
                   jax.ShapeDtypeStruct((n_state, n_seq), F32)],
        scratch_shapes=[pltpu.VMEM((N_PARTS, n_rows, LANES), F32),
                        pltpu.VMEM((n_rows, STATE_COLS), F32),
                        pltpu.VMEM((n_seq, STATE_COLS), F32),
                        pltpu.VMEM((2, n_rows, KV_WIDTH), F32)],
        compiler_params=_params(("arbitrary",)),
        name="sample_inproj",
    )(*ins)

    group = N_HEADS // N_KV_HEADS
    rows = n_steps * seq_batch
    sink_sample = jnp.repeat(w["sinks"].reshape(N_KV_HEADS, group), rows, axis=1)[..., None]
    kc = cache_k.transpose(0, 2, 3, 1)
    vc = cache_v.transpose(0, 2, 3, 1)
    step_block = lambda width: pl.BlockSpec((n_steps, seq_batch, width), lambda b: (0, b, 0))
    cache_block = pl.BlockSpec((seq_batch, N_KV_HEADS, HEAD_DIM, n_cache), lambda b: (b, 0, 0, 0))
    new_t_block = pl.BlockSpec((1, KV_WIDTH, rows), lambda b: (b, 0, 0))

    def per_batch(new_t):
        return new_t.reshape(KV_WIDTH, n_seq // seq_batch, rows).transpose(1, 0, 2)

    ins = [q.reshape(n_steps, n_seq, ATTN_WIDTH), k_new.reshape(n_steps, n_seq, KV_WIDTH),
           v_new.reshape(n_steps, n_seq, KV_WIDTH), per_batch(k_new_t), per_batch(v_new_t), kc, vc,
           sink_sample, x_tm, ssm_n, w["g_attn"], w["wo_a"], w["wo_s"]]
    in_specs = [step_block(ATTN_WIDTH), step_block(KV_WIDTH), step_block(KV_WIDTH), new_t_block,
                new_t_block, cache_block, cache_block] + [_const(a.shape) for a in ins[7:]]
    x1_tm, k_win_t, v_win_t = pl.pallas_call(
        functools.partial(_sample_attn_kernel, n_steps=n_steps, n_seq=n_seq, seq_batch=seq_batch,
                          n_cache=n_cache),
        grid=(n_seq // seq_batch,),
        in_specs=in_specs,
        out_specs=[_full((n_rows, D_MODEL)), cache_block, cache_block],
        out_shape=[jax.ShapeDtypeStruct((n_rows, D_MODEL), F32),
                   jax.ShapeDtypeStruct(kc.shape, F32), jax.ShapeDtypeStruct(vc.shape, F32)],
        scratch_shapes=[pltpu.VMEM((n_steps, n_seq, ATTN_WIDTH), F32),
                        pltpu.VMEM((N_QSLABS * rows, seq_batch * n_cache), F32),
                        pltpu.VMEM((N_QSLABS * rows, rows), F32)],
        compiler_params=_params(("arbitrary",)),
        name="sample_attn",
    )(*ins)

    conv0 = st_conv.transpose(1, 0, 2).reshape(2 * n_seq, D_FF)
    y_tm, conv_tm = _ffn_call(x1_tm, conv0, w, rows_per_step=n_seq, block_rows=n_rows)
    y_sample = y_tm.reshape(n_steps, n_seq, D_MODEL).transpose(1, 0, 2)
    s_conv = conv_tm.reshape(2, n_seq, D_FF).transpose(1, 0, 2)
    s_re = h_re.reshape(N_SSM_GROUPS, SSM_STATE, n_seq).transpose(2, 0, 1)
    s_im = h_im.reshape(N_SSM_GROUPS, SSM_STATE, n_seq).transpose(2, 0, 1)

    s_k = k_win_t.transpose(0, 3, 1, 2)
    s_v = v_win_t.transpose(0, 3, 1, 2)
    return y_sample, s_k[None], s_v[None], s_re[None], s_im[None], s_conv[None]


def kernel(x_prompt, x_sample, cache_k_win, cache_v_win, state_ssm_re, state_ssm_im, state_conv, meta_tokens, g_mix, w_in, sinks, lam_re, lam_im, log_dt, b_re, b_im, c_re, c_im, d_skip, w_glu, b_glu, g_attn_out, g_ssm_out, w_o, g_ffn, w_gate, w_up, conv_w, conv_b, w_down, g_final):
    assert g_mix.shape[0] == 1, "single-layer kernel"
    w = _prep_weights(g_mix[0], w_in[0], sinks[0], lam_re[0], lam_im[0], log_dt[0], b_re[0], b_im[0],
                      c_re[0], c_im[0], d_skip[0], w_glu[0], b_glu[0], g_attn_out[0], g_ssm_out[0],
                      w_o[0], g_ffn[0], w_gate[0], w_up[0], conv_w[0], conv_b[0], w_down[0], g_final)
    y_p, p_k, p_v, p_re, p_im, p_conv = _prompt_path(x_prompt, meta_tokens, w)
    y_s, s_k, s_v, s_re, s_im, s_conv = _sample_path(
        x_sample, cache_k_win[0], cache_v_win[0], state_ssm_re[0], state_ssm_im[0], state_conv[0], w)
    return (y_p, y_s, p_k, p_v, p_re, p_im, p_conv, s_k, s_v, s_re, s_im, s_conv)
```
